```python
import jax, jax.numpy as jnp
from jax import lax
import numpy as np

D_MODEL = 2048
BATCH = 8
SEQ = 4096
DEPTH = 4

CHUNK = 64
Q_BLOCK = 128
HEAD_DIM = 128
MEM_TOKENS = 256
MEM_HEADS = 4
MEM_W = MEM_HEADS * HEAD_DIM
MIX_W = D_MODEL
MAIN_W = MIX_W - MEM_W
GLA_HEADS = 4
GLA_KEY = MAIN_W // 2
GLA_HK = GLA_KEY // GLA_HEADS
GLA_HV = MAIN_W // GLA_HEADS
GLA_GATE_RANK = 16
GLA_GATE_NORMALIZER = 16.0
FOX_HEADS = MAIN_W // HEAD_DIM
D_FF = 256 * ((8 * D_MODEL // 3 + 255) // 256)
A_IN = 2 * GLA_KEY + MAIN_W + GLA_GATE_RANK + MAIN_W + MEM_W
B_IN = 2 * MAIN_W + MEM_W
KV_SHARED = 2 * MAIN_W + FOX_HEADS
EPS = 1e-6

kernel_name = "yoco_gla_fox_macaron_memory_trunk"


def _rmsnorm(x, g):
    xf = x.astype(jnp.float32)
    y = xf * lax.rsqrt(jnp.mean(xf * xf, axis=-1, keepdims=True) + EPS)
    return (y * g.astype(jnp.float32)).astype(x.dtype)


def _swiglu(h, w1, w3, w2):
    return (jax.nn.silu(h @ w1) * (h @ w3)) @ w2


def _gla_chunk_causal(q, k, v, log_a):
    b, s, h, dk = q.shape
    dv = v.shape[-1]
    n = s // CHUNK

    def to_chunks(t):
        return jnp.moveaxis(t.reshape(b, n, CHUNK, h, t.shape[-1]), 1, 0)

    qc, kc, vc = to_chunks(q), to_chunks(k), to_chunks(v)
    cum = jnp.cumsum(to_chunks(log_a), axis=2)
    total = cum[:, :, -1:]
    k_dec = (kc.astype(jnp.float32) * jnp.exp(total - cum)).astype(k.dtype)
    a_chunk = jnp.exp(total[:, :, 0])

    def step(state, inp):
        q_c, k_c, v_c, a_c = inp
        state = a_c[..., None] * state + jnp.einsum(
            'bchk,bchv->bhkv', k_c, v_c, preferred_element_type=jnp.float32)
        out = jnp.einsum('bchk,bhkv->bchv', q_c.astype(jnp.float32), state)
        return state, out.astype(v_c.dtype)

    state0 = jnp.zeros((b, h, dk, dv), jnp.float32)
    _, o = lax.scan(step, state0, (qc, k_dec, vc, a_chunk))
    return jnp.moveaxis(o, 0, 1).reshape(b, s, h, dv)


def _forgetting_attention(q, k, v, cum_log_f):
    b, s, h, d = q.shape
    scale = d ** -0.5
    outs = []
    for i in range(s // Q_BLOCK):
        q0 = i * Q_BLOCK
        q1 = q0 + Q_BLOCK
        logits = jnp.einsum('bqhd,bkhd->bhqk', q[:, q0:q1], k[:, :q1],
                            preferred_element_type=jnp.float32) * scale
        logits = logits + cum_log_f[:, :, q0:q1, None] - cum_log_f[:, :, None, :q1]
        causal = (q0 + jnp.arange(Q_BLOCK))[:, None] >= jnp.arange(q1)[None, :]
        p = jax.nn.softmax(jnp.where(causal, logits, -jnp.inf), axis=-1)
        outs.append(jnp.einsum('bhqk,bkhd->bqhd', p.astype(v.dtype), v[:, :q1]))
    return jnp.concatenate(outs, axis=1)


def _memory_attention(qm, mk, mv):
    logits = jnp.einsum('bshd,bmhd->bhsm', qm, mk,
                        preferred_element_type=jnp.float32) * (qm.shape[-1] ** -0.5)
    p = jax.nn.softmax(logits, axis=-1)
    return jnp.einsum('bhsm,bmhd->bshd', p.astype(mv.dtype), mv)


def setup_inputs(seed: int = 0) -> dict:
    key = jax.random.key(seed)
    ks = jax.random.split(key, 32)
    f32 = jnp.float32
    n_a = DEPTH // 2
    n_b = DEPTH - n_a

    def w(k, shape, fan_in, scale=1.0):
        return jax.random.normal(k, shape, f32) * (scale * fan_in ** -0.5)

    def gain(k, shape):
        return 1.0 + 0.05 * jax.random.normal(k, shape, f32)

    return {
        "x": jax.random.normal(ks[0], (BATCH, SEQ, D_MODEL), f32),
        "mem": jax.random.normal(ks[1], (BATCH, MEM_TOKENS, D_MODEL), f32),
        "ffn_norm": gain(ks[2], (DEPTH, 2, D_MODEL)),
        "ffn_w1": w(ks[3], (DEPTH, 2, D_MODEL, D_FF), D_MODEL),
        "ffn_w3": w(ks[4], (DEPTH, 2, D_MODEL, D_FF), D_MODEL),
        "ffn_w2": w(ks[5], (DEPTH, 2, D_FF, D_MODEL), D_FF),
        "mix_norm": gain(ks[6], (DEPTH, D_MODEL)),
        "mem_norm": gain(ks[7], (DEPTH, D_MODEL)),
        "w_mem_kv": w(ks[8], (DEPTH, D_MODEL, 2 * MEM_W), D_MODEL),
        "mem_q_norm": gain(ks[9], (DEPTH, HEAD_DIM)),
        "mem_k_norm": gain(ks[10], (DEPTH, HEAD_DIM)),
        "w_out": w(ks[11], (DEPTH, MIX_W, D_MODEL), MIX_W),
        "a_w_in": w(ks[12], (n_a, D_MODEL, A_IN), D_MODEL),
        "a_w_gate_up": w(ks[13], (n_a, GLA_GATE_RANK, GLA_KEY), GLA_GATE_RANK),
        "a_b_gate": 0.1 * jax.random.normal(ks[14], (n_a, GLA_KEY), f32),
        "a_out_norm": gain(ks[15], (n_a, GLA_HV)),
        "b_w_in": w(ks[16], (n_b, D_MODEL, B_IN), D_MODEL),
        "b_q_norm": gain(ks[17], (n_b, HEAD_DIM)),
        "kv_norm": gain(ks[18], (D_MODEL,)),
        "w_kv": w(ks[19], (D_MODEL, KV_SHARED), D_MODEL),
        "b_f": jax.random.uniform(ks[20], (FOX_HEADS,), f32, 1.0, 5.0),
        "k_norm": gain(ks[21], (HEAD_DIM,)),
    }


def reference(x, mem, ffn_norm, ffn_w1, ffn_w3, ffn_w2, mix_norm, mem_norm,
              w_mem_kv, mem_q_norm, mem_k_norm, w_out, a_w_in, a_w_gate_up,
              a_b_gate, a_out_norm, b_w_in, b_q_norm, kv_norm, w_kv, b_f, k_norm):
    bsz, seq, _ = x.shape
    n_a = a_w_in.shape[0]
    a_split = list(np.cumsum([GLA_KEY, GLA_KEY, MAIN_W, GLA_GATE_RANK, MAIN_W]))
    b_split = [MAIN_W, 2 * MAIN_W]
    ks_shared = vs_shared = cum_shared = None

    for l in range(DEPTH):
        if l == n_a:
            hs = _rmsnorm(x, kv_norm)
            k_s, v_s, f_s = jnp.split(hs @ w_kv, b_split, axis=-1)
            ks_shared = _rmsnorm(k_s.reshape(bsz, seq, FOX_HEADS, HEAD_DIM), k_norm)
            vs_shared = v_s.reshape(bsz, seq, FOX_HEADS, HEAD_DIM)
            log_f = jax.nn.log_sigmoid((f_s + b_f).astype(jnp.float32))
            cum_shared = jnp.moveaxis(jnp.cumsum(log_f, axis=1), 1, 2)

        x = x + 0.5 * _swiglu(_rmsnorm(x, ffn_norm[l, 0]), ffn_w1[l, 0],
                              ffn_w3[l, 0], ffn_w2[l, 0])

        h = _rmsnorm(x, mix_norm[l])
        if l < n_a:
            q, k, v, lr, g, qm = jnp.split(h @ a_w_in[l], a_split, axis=-1)
            q = q.reshape(bsz, seq, GLA_HEADS, GLA_HK) * (GLA_HK ** -0.5)
            k = k.reshape(bsz, seq, GLA_HEADS, GLA_HK)
            v = v.reshape(bsz, seq, GLA_HEADS, GLA_HV)
            log_a = jax.nn.log_sigmoid(
                (lr @ a_w_gate_up[l] + a_b_gate[l]).astype(jnp.float32)) / GLA_GATE_NORMALIZER
            log_a = log_a.reshape(bsz, seq, GLA_HEADS, GLA_HK)
            o = _gla_chunk_causal(q, k, v, log_a)
            o = _rmsnorm(o, a_out_norm[l]) * jax.nn.silu(g.reshape(bsz, seq, GLA_HEADS, GLA_HV))
        else:
            j = l - n_a
            q, g, qm = jnp.split(h @ b_w_in[j], b_split, axis=-1)
            q = _rmsnorm(q.reshape(bsz, seq, FOX_HEADS, HEAD_DIM), b_q_norm[j])
            o = _forgetting_attention(q, ks_shared, vs_shared, cum_shared)
            o = o * jax.nn.sigmoid(g.reshape(bsz, seq, FOX_HEADS, HEAD_DIM))
        o = o.reshape(bsz, seq, MAIN_W)

        mh = _rmsnorm(mem, mem_norm[l])
        mkv = (mh @ w_mem_kv[l]).reshape(bsz, mem.shape[1], 2, MEM_HEADS, HEAD_DIM)
        mk = _rmsnorm(mkv[:, :, 0], mem_k_norm[l])
        mv = mkv[:, :, 1]
        qm = _rmsnorm(qm.reshape(bsz, seq, MEM_HEADS, HEAD_DIM), mem_q_norm[l])
        mo = _memory_attention(qm, mk, mv).reshape(bsz, seq, MEM_W)

        x = x + jnp.concatenate([o, mo], axis=-1) @ w_out[l]

        x = x + 0.5 * _swiglu(_rmsnorm(x, ffn_norm[l, 1]), ffn_w1[l, 1],
                              ffn_w3[l, 1], ffn_w2[l, 1])
    return x
```

```python
import functools

import numpy as np
import jax
import jax.numpy as jnp
from jax import lax
from jax.experimental import pallas as pl
from jax.experimental.pallas import tpu as pltpu

F32 = jnp.float32
BF16 = jnp.bfloat16

CHUNK = 64
HEAD_DIM = 128
MEM_HEADS = 4
MEM_W = MEM_HEADS * HEAD_DIM
GLA_HEADS = 4
GLA_GATE_RANK = 16
GLA_GATE_NORMALIZER = 16.0
EPS = 1e-6

LANES = 128
V7X_VMEM_LIMIT_BYTES = 56 * 1024 * 1024

GLA_HK_PAD = 256


def _rms(x, gain):
    return x * lax.rsqrt(jnp.mean(x * x, axis=-1, keepdims=True) + EPS) * gain


def _log_sigmoid(x):
    return jnp.minimum(x, 0.0) - jnp.log1p(jnp.exp(-jnp.abs(x)))


def _silu(x):
    return x * jax.nn.sigmoid(x)


def _cumsum_rows(x):
    n = x.shape[0]
    row = lax.broadcasted_iota(jnp.int32, (n, n), 0)
    col = lax.broadcasted_iota(jnp.int32, (n, n), 1)
    tri = (row >= col).astype(BF16)
    hi = x.astype(BF16)
    r1 = x - hi.astype(F32)
    mid = r1.astype(BF16)
    lo = (r1 - mid.astype(F32)).astype(BF16)
    dot = functools.partial(jnp.dot, preferred_element_type=F32)
    return dot(tri, hi) + dot(tri, mid) + dot(tri, lo)


def _params(*semantics):
    return pltpu.CompilerParams(dimension_semantics=semantics,
                                vmem_limit_bytes=V7X_VMEM_LIMIT_BYTES)


def _ffn_body(x_ref, g_ref, w1_ref, w3_ref, w2_ref, o_ref, h_scr):
    @pl.when(pl.program_id(1) == 0)
    def _():
        x = x_ref[...]
        h_scr[...] = _rms(x, g_ref[...]).astype(BF16)
        o_ref[...] = x

    h = h_scr[...]
    a = jnp.dot(h, w1_ref[...], preferred_element_type=F32)
    b = jnp.dot(h, w3_ref[...], preferred_element_type=F32)
    act = (_silu(a) * b * 0.5).astype(BF16)
    o_ref[...] += jnp.dot(act, w2_ref[...], preferred_element_type=F32)


def _ffn(x, gain, w1, w3, w2, *, tm, tf):
    n, d = x.shape
    f = w1.shape[1]
    return pl.pallas_call(
        _ffn_body,
        grid=(n // tm, f // tf),
        in_specs=[
            pl.BlockSpec((tm, d), lambda i, j: (i, 0)),
            pl.BlockSpec((1, d), lambda i, j: (0, 0)),
            pl.BlockSpec((d, tf), lambda i, j: (0, j)),
            pl.BlockSpec((d, tf), lambda i, j: (0, j)),
            pl.BlockSpec((tf, d), lambda i, j: (j, 0)),
        ],
        out_specs=pl.BlockSpec((tm, d), lambda i, j: (i, 0)),
        out_shape=jax.ShapeDtypeStruct((n, d), F32),
        scratch_shapes=[pltpu.VMEM((tm, d), BF16)],
        compiler_params=_params("parallel", "arbitrary"),
        name="ffn",
    )(x, gain.reshape(1, d), w1, w3, w2)


def _norm_proj_body(*refs, groups, n_extra):
    x_ref, g_ref, w_ref = refs[:3]
    extra = refs[3:3 + n_extra]
    outs = refs[3 + n_extra:3 + n_extra + len(groups)]
    h_scr = refs[-1]
    j = pl.program_id(1)

    @pl.when(j == 0)
    def _():
        h_scr[...] = _rms(x_ref[...], g_ref[...]).astype(BF16)

    acc = jnp.dot(h_scr[...], w_ref[...], preferred_element_type=F32)
    start = 0
    for (n_tiles, _, _, fn), o_ref in zip(groups, outs):
        @pl.when((j >= start) & (j < start + n_tiles))
        def _(fn=fn, o_ref=o_ref):
            o_ref[...] = fn(acc, extra).astype(o_ref.dtype)
        start += n_tiles


def _norm_proj(x, gain, w, groups, extras, *, tm, tn, name):
    n, d = x.shape
    assert w.shape[1] == tn * sum(g[0] for g in groups)
    in_specs = [
        pl.BlockSpec((tm, d), lambda i, j: (i, 0)),
        pl.BlockSpec((1, d), lambda i, j: (0, 0)),
        pl.BlockSpec((d, tn), lambda i, j: (0, j)),
    ] + [pl.BlockSpec(e.shape, lambda i, j: (0, 0)) for e in extras]
    out_specs, out_shapes = [], []
    start = 0
    for n_tiles, width, dtype, _ in groups:
        out_specs.append(pl.BlockSpec(
            (tm, width),
            lambda i, j, s=start, c=n_tiles: (i, jnp.clip(j - s, 0, c - 1))))
        out_shapes.append(jax.ShapeDtypeStruct((n, width * n_tiles), dtype))
        start += n_tiles
    return pl.pallas_call(
        functools.partial(_norm_proj_body, groups=groups, n_extra=len(extras)),
        grid=(n // tm, start),
        in_specs=in_specs,
        out_specs=out_specs,
        out_shape=out_shapes,
        scratch_shapes=[pltpu.VMEM((tm, d), BF16)],
        compiler_params=_params("parallel", "arbitrary"),
        name=name,
    )(x, gain.reshape(1, d), w, *extras)


def _scale_ep(scale):
    return lambda acc, extra: acc * scale


def _plain_ep(acc, extra):
    return acc


def _head_norm_ep(gain_idx, scale):
    def fn(acc, extra):
        gain = extra[gain_idx][...]
        parts = []
        for h in range(acc.shape[1] // HEAD_DIM):
            a = acc[:, h * HEAD_DIM:(h + 1) * HEAD_DIM]
            parts.append(_rms(a, gain) * scale)
        return jnp.concatenate(parts, axis=-1)
    return fn


def _gla_gate_ep(w_up_idx, bias_idx):
    def fn(acc, extra):
        z = jnp.dot(acc.astype(BF16), extra[w_up_idx][...], preferred_element_type=F32)
        return _log_sigmoid(z + extra[bias_idx][...]) / GLA_GATE_NORMALIZER
    return fn


def _forget_gate_ep(bias_idx):
    def fn(acc, extra):
        return _log_sigmoid(acc[:, :LANES] + extra[bias_idx][...])
    return fn


def _cumsum_body(lf_ref, col_ref, row_ref, carry):
    @pl.when(pl.program_id(1) == 0)
    def _():
        carry[...] = jnp.zeros_like(carry)

    cum = _cumsum_rows(lf_ref[...]) + carry[...]
    col_ref[...] = cum
    row_ref[...] = cum.T[:row_ref.shape[0], :]
    carry[...] = cum[cum.shape[0] - 1:, :]


def _seq_cumsum(logf, bsz, seq, *, ts, n_rows):
    return pl.pallas_call(
        _cumsum_body,
        grid=(bsz, seq // ts),
        in_specs=[pl.BlockSpec((None, ts, LANES), lambda b, s: (b, s, 0))],
        out_specs=[pl.BlockSpec((None, ts, LANES), lambda b, s: (b, s, 0)),
                   pl.BlockSpec((None, n_rows, ts), lambda b, s: (b, 0, s))],
        out_shape=[jax.ShapeDtypeStruct((bsz, seq, LANES), F32),
                   jax.ShapeDtypeStruct((bsz, n_rows, seq), F32)],
        scratch_shapes=[pltpu.VMEM((1, LANES), F32)],
        compiler_params=_params("parallel", "arbitrary"),
        name="forget_cumsum",
    )(logf.reshape(bsz, seq, LANES))


def _gla_body(q_ref, k_ref, la_ref, v_ref, g_ref, gain_ref, o_ref, st_ref, *, n_chunk, hv):
    @pl.when(pl.program_id(1) == 0)
    def _():
        st_ref[...] = jnp.zeros_like(st_ref)

    gain = gain_ref[...]
    for c in range(n_chunk):
        rows = pl.ds(c * CHUNK, CHUNK)
        cum = _cumsum_rows(la_ref[rows, :])
        total = cum[CHUNK - 1:, :]
        k_dec = (k_ref[rows, :] * jnp.exp(total - cum)).astype(BF16)
        a_chunk = jnp.exp(total)
        for h in range(GLA_HEADS):
            ks = slice(h * GLA_HK_PAD, (h + 1) * GLA_HK_PAD)
            vs = slice(h * hv, (h + 1) * hv)
            upd = lax.dot_general(v_ref[rows, vs], k_dec[:, ks],
                                  (((0,), (0,)), ((), ())), preferred_element_type=F32)
            st = a_chunk[:, ks] * st_ref[h] + upd
            st_ref[h] = st
            o = lax.dot_general(q_ref[rows, ks], st.astype(BF16),
                                (((1,), (1,)), ((), ())), preferred_element_type=F32)
            o = _rms(o, gain) * _silu(g_ref[rows, vs].astype(F32))
            o_ref[rows, vs] = o.astype(BF16)


def _gla(q, k, la, v, g, gain, bsz, seq, *, n_chunk):
    hv = v.shape[1] // GLA_HEADS
    kw = q.shape[1]
    tc = n_chunk * CHUNK
    nt = seq // tc

    def spec(width):
        return pl.BlockSpec((tc, width), lambda b, c: (b * nt + c, 0))

    return pl.pallas_call(
        functools.partial(_gla_body, n_chunk=n_chunk, hv=hv),
        grid=(bsz, nt),
        in_specs=[spec(kw), spec(kw), spec(kw), spec(v.shape[1]), spec(v.shape[1]),
                  pl.BlockSpec((1, hv), lambda b, c: (0, 0))],
        out_specs=spec(v.shape[1]),
        out_shape=jax.ShapeDtypeStruct(v.shape, BF16),
        scratch_shapes=[pltpu.VMEM((GLA_HEADS, hv, GLA_HK_PAD), F32)],
        compiler_params=_params("parallel", "arbitrary"),
        name="gla",
    )(q, k, la, v, g, gain.reshape(1, hv))


def _fox_body(qi_tab, ki_tab, q_ref, k_ref, v_ref, crow_ref, ccol_ref, g_ref, o_ref,
              m_scr, l_scr, acc_scr, cc_scr, *, tq, tk):
    t = pl.program_id(2)
    h = pl.program_id(1)
    qi = qi_tab[t]
    ki = ki_tab[t]

    @pl.when(ki == 0)
    def _():
        m_scr[...] = jnp.full_like(m_scr, -jnp.inf)
        l_scr[...] = jnp.zeros_like(l_scr)
        acc_scr[...] = jnp.zeros_like(acc_scr)
        blk = ccol_ref[...]
        lane = lax.broadcasted_iota(jnp.int32, blk.shape, 1)
        cc_scr[...] = jnp.sum(jnp.where(lane == h, blk, 0.0), axis=1, keepdims=True)

    s = lax.dot_general(q_ref[...], k_ref[...], (((1,), (1,)), ((), ())),
                        preferred_element_type=F32)
    s = s + (cc_scr[...] - crow_ref[...])

    def mask(s):
        row = qi * tq + lax.broadcasted_iota(jnp.int32, s.shape, 0)
        col = ki * tk + lax.broadcasted_iota(jnp.int32, s.shape, 1)
        return jnp.where(row >= col, s, -jnp.inf)

    s = lax.cond((ki + 1) * tk - 1 > qi * tq, mask, lambda s: s, s)
    m_prev = m_scr[...]
    m_new = jnp.maximum(m_prev, jnp.max(s, axis=1, keepdims=True))
    alpha = jnp.exp(m_prev - m_new)
    p = jnp.exp(s - m_new)
    l_scr[...] = alpha * l_scr[...] + jnp.sum(p, axis=1, keepdims=True)
    acc_scr[...] = alpha * acc_scr[...] + jnp.dot(p.astype(BF16), v_ref[...],
                                                  preferred_element_type=F32)
    m_scr[...] = m_new

    @pl.when((ki + 1) * tk >= (qi + 1) * tq)
    def _():
        o = acc_scr[...] / l_scr[...]
        o_ref[...] = (o * jax.nn.sigmoid(g_ref[...].astype(F32))).astype(BF16)


def _fox(q, k, v, cum_col, cum_row, g, *, tq, tk):
    bsz, seq, width = q.shape
    heads = width // HEAD_DIM
    n_rows = cum_row.shape[1]
    pairs = [(qi, ki) for qi in range(seq // tq) for ki in range(((qi + 1) * tq) // tk)]
    qi_tab = jnp.asarray(np.array([p[0] for p in pairs], np.int32))
    ki_tab = jnp.asarray(np.array([p[1] for p in pairs], np.int32))
    q_spec = pl.BlockSpec((None, tq, HEAD_DIM), lambda b, h, t, qt, kt: (b, qt[t], h))
    kv_spec = pl.BlockSpec((None, tk, HEAD_DIM), lambda b, h, t, qt, kt: (b, kt[t], h))
    grid_spec = pltpu.PrefetchScalarGridSpec(
        num_scalar_prefetch=2,
        grid=(bsz, heads, len(pairs)),
        in_specs=[
            q_spec, kv_spec, kv_spec,
            pl.BlockSpec((None, None, 1, tk), lambda b, h, t, qt, kt: (b, h, 0, kt[t])),
            pl.BlockSpec((None, tq, LANES), lambda b, h, t, qt, kt: (b, qt[t], 0)),
            q_spec,
        ],
        out_specs=q_spec,
        scratch_shapes=[pltpu.VMEM((tq, 1), F32), pltpu.VMEM((tq, 1), F32),
                        pltpu.VMEM((tq, HEAD_DIM), F32), pltpu.VMEM((tq, 1), F32)],
    )
    return pl.pallas_call(
        functools.partial(_fox_body, tq=tq, tk=tk),
        grid_spec=grid_spec,
        out_shape=jax.ShapeDtypeStruct(q.shape, BF16),
        compiler_params=_params("parallel", "parallel", "arbitrary"),
        name="fox",
    )(qi_tab, ki_tab, q, k, v, cum_row.reshape(bsz, n_rows, 1, seq), cum_col, g)


def _mem_attn_body(q_ref, mk_ref, mv_ref, o_ref):
    for h in range(MEM_HEADS):
        hs = slice(h * HEAD_DIM, (h + 1) * HEAD_DIM)
        s = lax.dot_general(q_ref[:, hs], mk_ref[:, hs], (((1,), (1,)), ((), ())),
                            preferred_element_type=F32)
        e = jnp.exp(s - jnp.max(s, axis=1, keepdims=True))
        p = e / jnp.sum(e, axis=1, keepdims=True)
        o_ref[:, hs] = jnp.dot(p.astype(BF16), mv_ref[:, hs],
                               preferred_element_type=F32).astype(BF16)


def _mem_attn(qm, mk, mv, bsz, seq, *, tm):
    n_mem = mk.shape[0] // bsz
    nt = seq // tm
    return pl.pallas_call(
        _mem_attn_body,
        grid=(bsz, nt),
        in_specs=[pl.BlockSpec((tm, MEM_W), lambda b, i: (b * nt + i, 0)),
                  pl.BlockSpec((n_mem, MEM_W), lambda b, i: (b, 0)),
                  pl.BlockSpec((n_mem, MEM_W), lambda b, i: (b, 0))],
        out_specs=pl.BlockSpec((tm, MEM_W), lambda b, i: (b * nt + i, 0)),
        out_shape=jax.ShapeDtypeStruct(qm.shape, BF16),
        compiler_params=_params("parallel", "arbitrary"),
        name="mem_attn",
    )(qm, mk, mv)


def _out_proj_body(x_ref, o_ref, mo_ref, wo_ref, wm_ref, y_ref):
    y = x_ref[...] + jnp.dot(o_ref[...], wo_ref[...], preferred_element_type=F32)
    y_ref[...] = y + jnp.dot(mo_ref[...], wm_ref[...], preferred_element_type=F32)


def _out_proj(x, o, mo, w_o, w_m, *, tm):
    n, d = x.shape
    return pl.pallas_call(
        _out_proj_body,
        grid=(n // tm,),
        in_specs=[pl.BlockSpec((tm, d), lambda i: (i, 0)),
                  pl.BlockSpec((tm, o.shape[1]), lambda i: (i, 0)),
                  pl.BlockSpec((tm, mo.shape[1]), lambda i: (i, 0)),
                  pl.BlockSpec(w_o.shape, lambda i: (0, 0)),
                  pl.BlockSpec(w_m.shape, lambda i: (0, 0))],
        out_specs=pl.BlockSpec((tm, d), lambda i: (i, 0)),
        out_shape=jax.ShapeDtypeStruct((n, d), F32),
        compiler_params=_params("parallel"),
        name="out_proj",
    )(x, o, mo, w_o, w_m)


def _pad_cols(w, width):
    return jnp.pad(w, ((0, 0), (0, width - w.shape[1])))


def _pad_gla_heads(w, hk):
    lead = w.shape[:-1]
    w = w.reshape(lead + (GLA_HEADS, hk))
    w = jnp.pad(w, [(0, 0)] * len(lead) + [(0, 0), (0, GLA_HK_PAD - hk)])
    return w.reshape(lead + (GLA_HEADS * GLA_HK_PAD,))


def _forward(x, mem, ffn_norm, ffn_w1, ffn_w3, ffn_w2, mix_norm, mem_norm, w_mem_kv,
             mem_q_norm, mem_k_norm, w_out, a_w_in, a_w_gate_up, a_b_gate, a_out_norm,
             b_w_in, b_q_norm, kv_norm, w_kv, b_f, k_norm, *, tiles):
    bsz, seq, d = x.shape
    n = bsz * seq
    depth = ffn_norm.shape[0]
    n_a = a_w_in.shape[0]
    main_w = d - MEM_W
    gla_key = main_w // 2
    hk = gla_key // GLA_HEADS
    fox_heads = main_w // HEAD_DIM
    tn = tiles["proj_tn"]
    attn_scale = HEAD_DIM ** -0.5

    w1 = ffn_w1.astype(BF16)
    w3 = ffn_w3.astype(BF16)
    w2 = ffn_w2.astype(BF16)
    x2 = x.reshape(n, d)
    mem2 = mem.reshape(bsz * mem.shape[1], d)

    def ffn(x2, l, s):
        return _ffn(x2, ffn_norm[l, s], w1[l, s], w3[l, s], w2[l, s],
                    tm=tiles["ffn_tm"], tf=tiles["ffn_tf"])

    k_sh = v_sh = cum_col = cum_row = None
    for l in range(depth):
        if l == n_a:
            w = jnp.concatenate([w_kv[:, :2 * main_w], _pad_cols(w_kv[:, 2 * main_w:], tn)],
                                axis=1).astype(BF16)
            bias = _pad_cols(b_f.reshape(1, fox_heads), LANES)
            k_sh, v_sh, logf = _norm_proj(
                x2, kv_norm, w,
                [(main_w // tn, tn, BF16, _head_norm_ep(0, 1.0)),
                 (main_w // tn, tn, BF16, _plain_ep),
                 (1, LANES, F32, _forget_gate_ep(1))],
                [k_norm.reshape(1, HEAD_DIM), bias],
                tm=tiles["proj_tm"], tn=tn, name="kv_proj")
            cum_col, cum_row = _seq_cumsum(logf, bsz, seq, ts=tiles["cum_ts"],
                                           n_rows=-(-fox_heads // 8) * 8)

        x2 = ffn(x2, l, 0)

        mk, mv = _norm_proj(
            mem2, mem_norm[l], w_mem_kv[l].astype(BF16),
            [(MEM_W // tn, tn, BF16, _head_norm_ep(0, 1.0)),
             (MEM_W // tn, tn, BF16, _plain_ep)],
            [mem_k_norm[l].reshape(1, HEAD_DIM)],
            tm=tiles["mem_tm"], tn=tn, name="mem_kv_proj")
        qm_ep = _head_norm_ep(0, attn_scale)

        if l < n_a:
            wi = a_w_in[l]
            c = np.cumsum([0, gla_key, gla_key, main_w, GLA_GATE_RANK, main_w, MEM_W])
            w = jnp.concatenate([
                _pad_gla_heads(wi[:, c[0]:c[1]], hk),
                _pad_gla_heads(wi[:, c[1]:c[2]], hk),
                wi[:, c[2]:c[3]],
                wi[:, c[4]:c[5]],
                wi[:, c[5]:c[6]],
                _pad_cols(wi[:, c[3]:c[4]], tn),
            ], axis=1).astype(BF16)
            kw = GLA_HEADS * GLA_HK_PAD
            w_up = jnp.pad(_pad_gla_heads(a_w_gate_up[l], hk),
                           ((0, tn - GLA_GATE_RANK), (0, 0))).astype(BF16)
            b_gate = _pad_gla_heads(a_b_gate[l].reshape(1, gla_key), hk)
            q, k, v, g, qm, la = _norm_proj(
                x2, mix_norm[l], w,
                [(kw // tn, tn, BF16, _scale_ep(hk ** -0.5)),
                 (kw // tn, tn, F32, _plain_ep),
                 (main_w // tn, tn, BF16, _plain_ep),
                 (main_w // tn, tn, BF16, _plain_ep),
                 (MEM_W // tn, tn, BF16, qm_ep),
                 (1, kw, F32, _gla_gate_ep(1, 2))],
                [mem_q_norm[l].reshape(1, HEAD_DIM), w_up, b_gate],
                tm=tiles["proj_tm"], tn=tn, name="gla_in_proj")
            o = _gla(q, k, la, v, g, a_out_norm[l], bsz, seq, n_chunk=tiles["gla_chunks"])
        else:
            jl = l - n_a
            q, g, qm = _norm_proj(
                x2, mix_norm[l], b_w_in[jl].astype(BF16),
                [(main_w // tn, tn, BF16, _head_norm_ep(0, attn_scale)),
                 (main_w // tn, tn, BF16, _plain_ep),
                 (MEM_W // tn, tn, BF16, _head_norm_ep(1, attn_scale))],
                [b_q_norm[jl].reshape(1, HEAD_DIM), mem_q_norm[l].reshape(1, HEAD_DIM)],
                tm=tiles["proj_tm"], tn=tn, name="fox_in_proj")
            o = _fox(q.reshape(bsz, seq, main_w), k_sh.reshape(bsz, seq, main_w),
                     v_sh.reshape(bsz, seq, main_w), cum_col, cum_row,
                     g.reshape(bsz, seq, main_w), tq=tiles["fox_tq"], tk=tiles["fox_tk"])
            o = o.reshape(n, main_w)

        mo = _mem_attn(qm, mk, mv, bsz, seq, tm=tiles["attn_tm"])
        wo = w_out[l].astype(BF16)
        x2 = _out_proj(x2, o, mo, wo[:main_w], wo[main_w:], tm=tiles["out_tm"])
        x2 = ffn(x2, l, 1)
    return x2.reshape(bsz, seq, d)


_TILES = dict(ffn_tm=512, ffn_tf=512, proj_tm=1024, proj_tn=512, mem_tm=512, cum_ts=512,
              gla_chunks=8, fox_tq=512, fox_tk=512, attn_tm=1024, out_tm=512)


def kernel(x, mem, ffn_norm, ffn_w1, ffn_w3, ffn_w2, mix_norm, mem_norm, w_mem_kv, mem_q_norm, mem_k_norm, w_out, a_w_in, a_w_gate_up, a_b_gate, a_out_norm, b_w_in, b_q_norm, kv_norm, w_kv, b_f, k_norm):
    return _forward(x, mem, ffn_norm, ffn_w1, ffn_w3, ffn_w2, mix_norm, mem_norm, w_mem_kv,
                    mem_q_norm, mem_k_norm, w_out, a_w_in, a_w_gate_up, a_b_gate, a_out_norm,
                    b_w_in, b_q_norm, kv_norm, w_kv, b_f, k_norm, tiles=_TILES)
```

```python
import functools

import numpy as np
import jax
import jax.numpy as jnp
from jax import lax
from jax.experimental import pallas as pl
from jax.experimental.pallas import tpu as pltpu

F32 = jnp.float32
BF16 = jnp.bfloat16

CHUNK = 64
HEAD_DIM = 128
MEM_HEADS = 4
MEM_W = MEM_HEADS * HEAD_DIM
GLA_HEADS = 4
GLA_GATE_RANK = 16
GLA_GATE_NORMALIZER = 16.0
EPS = 1e-6
LOG2E = 1.4426950408889634
DECAY_TERMS = 3

LANES = 128
V7X_VMEM_LIMIT_BYTES = 56 * 1024 * 1024

GLA_HK_PAD = 256


def _rms(x, gain):
    return x * lax.rsqrt(jnp.mean(x * x, axis=-1, keepdims=True) + EPS) * gain


def _log_sigmoid(x):
    return jnp.minimum(x, 0.0) - jnp.log1p(jnp.exp(-jnp.abs(x)))


def _silu(x):
    return x * jax.nn.sigmoid(x)


def _split3(x):
    hi = x.astype(BF16)
    r1 = x - hi.astype(F32)
    mid = r1.astype(BF16)
    lo = (r1 - mid.astype(F32)).astype(BF16)
    return hi, mid, lo


def _cumsum_rows(x):
    n = x.shape[0]
    row = lax.broadcasted_iota(jnp.int32, (n, n), 0)
    col = lax.broadcasted_iota(jnp.int32, (n, n), 1)
    tri = (row >= col).astype(BF16)
    return sum(jnp.dot(tri, term, preferred_element_type=F32) for term in _split3(x))


def _params(*semantics):
    return pltpu.CompilerParams(dimension_semantics=semantics,
                                vmem_limit_bytes=V7X_VMEM_LIMIT_BYTES)


def _ffn_body(x_ref, g_ref, w1_ref, w3_ref, w2_ref, o_ref, h_scr):
    @pl.when(pl.program_id(1) == 0)
    def _():
        x = x_ref[...]
        h_scr[...] = _rms(x, g_ref[...]).astype(BF16)
        o_ref[...] = x

    h = h_scr[...]
    a = jnp.dot(h, w1_ref[...], preferred_element_type=F32)
    b = jnp.dot(h, w3_ref[...], preferred_element_type=F32)
    act = (_silu(a) * b * 0.5).astype(BF16)
    o_ref[...] += jnp.dot(act, w2_ref[...], preferred_element_type=F32)


def _ffn(x, gain, w1, w3, w2, *, tm, tf):
    n, d = x.shape
    f = w1.shape[1]
    return pl.pallas_call(
        _ffn_body,
        grid=(n // tm, f // tf),
        in_specs=[
            pl.BlockSpec((tm, d), lambda i, j: (i, 0)),
            pl.BlockSpec((1, d), lambda i, j: (0, 0)),
            pl.BlockSpec((d, tf), lambda i, j: (0, j)),
            pl.BlockSpec((d, tf), lambda i, j: (0, j)),
            pl.BlockSpec((tf, d), lambda i, j: (j, 0)),
        ],
        out_specs=pl.BlockSpec((tm, d), lambda i, j: (i, 0)),
        out_shape=jax.ShapeDtypeStruct((n, d), F32),
        scratch_shapes=[pltpu.VMEM((tm, d), BF16)],
        compiler_params=_params("parallel", "arbitrary"),
        name="ffn",
    )(x, gain.reshape(1, d), w1, w3, w2)


def _norm_proj_body(*refs, groups, n_extra):
    x_ref, g_ref, w_ref = refs[:3]
    extra = refs[3:3 + n_extra]
    outs = refs[3 + n_extra:3 + n_extra + len(groups)]
    h_scr = refs[-1]
    j = pl.program_id(1)

    @pl.when(j == 0)
    def _():
        h_scr[...] = _rms(x_ref[...], g_ref[...]).astype(BF16)

    acc = jnp.dot(h_scr[...], w_ref[...], preferred_element_type=F32)
    start = 0
    for (n_tiles, _, _, fn), o_ref in zip(groups, outs):
        @pl.when((j >= start) & (j < start + n_tiles))
        def _(fn=fn, o_ref=o_ref):
            o_ref[...] = fn(acc, extra).astype(o_ref.dtype)
        start += n_tiles


def _norm_proj(x, gain, w, groups, extras, *, tm, tn, name):
    n, d = x.shape
    assert w.shape[1] == tn * sum(g[0] for g in groups)
    in_specs = [
        pl.BlockSpec((tm, d), lambda i, j: (i, 0)),
        pl.BlockSpec((1, d), lambda i, j: (0, 0)),
        pl.BlockSpec((d, tn), lambda i, j: (0, j)),
    ] + [pl.BlockSpec(e.shape, lambda i, j: (0, 0)) for e in extras]
    out_specs, out_shapes = [], []
    start = 0
    for n_tiles, width, dtype, _ in groups:
        out_specs.append(pl.BlockSpec(
            (tm, width),
            lambda i, j, s=start, c=n_tiles: (i, jnp.clip(j - s, 0, c - 1))))
        out_shapes.append(jax.ShapeDtypeStruct((n, width * n_tiles), dtype))
        start += n_tiles
    return pl.pallas_call(
        functools.partial(_norm_proj_body, groups=groups, n_extra=len(extras)),
        grid=(n // tm, start),
        in_specs=in_specs,
        out_specs=out_specs,
        out_shape=out_shapes,
        scratch_shapes=[pltpu.VMEM((tm, d), BF16)],
        compiler_params=_params("parallel", "arbitrary"),
        name=name,
    )(x, gain.reshape(1, d), w, *extras)


def _scale_ep(scale):
    return lambda acc, extra: acc * scale


def _plain_ep(acc, extra):
    return acc


def _head_norm_ep(gain_idx, scale):
    def fn(acc, extra):
        gain = extra[gain_idx][...]
        parts = []
        for h in range(acc.shape[1] // HEAD_DIM):
            a = acc[:, h * HEAD_DIM:(h + 1) * HEAD_DIM]
            parts.append(_rms(a, gain) * scale)
        return jnp.concatenate(parts, axis=-1)
    return fn


def _gla_gate_ep(w_up_idx, bias_idx):
    def fn(acc, extra):
        z = jnp.dot(acc.astype(BF16), extra[w_up_idx][...], preferred_element_type=F32)
        return _log_sigmoid(z + extra[bias_idx][...]) / GLA_GATE_NORMALIZER
    return fn


def _forget_gate_ep(bias_idx):
    def fn(acc, extra):
        return _log_sigmoid(acc[:, :LANES] + extra[bias_idx][...])
    return fn


def _cumsum_body(lf_ref, col_ref, kd_ref, carry):
    @pl.when(pl.program_id(1) == 0)
    def _():
        carry[...] = jnp.zeros_like(carry)

    cum = _cumsum_rows(lf_ref[...]) + carry[...]
    carry[...] = cum[cum.shape[0] - 1:, :]
    c2 = cum * LOG2E
    col_ref[...] = c2
    terms = [term.astype(F32) for term in _split3(-c2)]
    lane = lax.broadcasted_iota(jnp.int32, c2.shape, 1)
    for h in range(kd_ref.shape[0]):
        tile = jnp.zeros_like(c2)
        for i, term in enumerate(terms):
            tile = jnp.where(lane == i, term[:, h:h + 1], tile)
        kd_ref[h] = tile.astype(BF16)


def _seq_cumsum(logf, bsz, seq, heads, *, ts):
    return pl.pallas_call(
        _cumsum_body,
        grid=(bsz, seq // ts),
        in_specs=[pl.BlockSpec((None, ts, LANES), lambda b, s: (b, s, 0))],
        out_specs=[pl.BlockSpec((None, ts, LANES), lambda b, s: (b, s, 0)),
                   pl.BlockSpec((None, heads, ts, LANES), lambda b, s: (b, 0, s, 0))],
        out_shape=[jax.ShapeDtypeStruct((bsz, seq, LANES), F32),
                   jax.ShapeDtypeStruct((bsz, heads, seq, LANES), BF16)],
        scratch_shapes=[pltpu.VMEM((1, LANES), F32)],
        compiler_params=_params("parallel", "arbitrary"),
        name="forget_cumsum",
    )(logf.reshape(bsz, seq, LANES))


def _gla_body(q_ref, k_ref, la_ref, v_ref, g_ref, gain_ref, o_ref, st_ref, *, n_chunk, hv):
    @pl.when(pl.program_id(1) == 0)
    def _():
        st_ref[...] = jnp.zeros_like(st_ref)

    gain = gain_ref[...]
    for c in range(n_chunk):
        rows = pl.ds(c * CHUNK, CHUNK)
        cum = _cumsum_rows(la_ref[rows, :])
        total = cum[CHUNK - 1:, :]
        k_dec = (k_ref[rows, :] * jnp.exp(total - cum)).astype(BF16)
        a_chunk = jnp.exp(total)
        for h in range(GLA_HEADS):
            ks = slice(h * GLA_HK_PAD, (h + 1) * GLA_HK_PAD)
            vs = slice(h * hv, (h + 1) * hv)
            upd = lax.dot_general(v_ref[rows, vs], k_dec[:, ks],
                                  (((0,), (0,)), ((), ())), preferred_element_type=F32)
            st = a_chunk[:, ks] * st_ref[h] + upd
            st_ref[h] = st
            o = lax.dot_general(q_ref[rows, ks], st.astype(BF16),
                                (((1,), (1,)), ((), ())), preferred_element_type=F32)
            o = _rms(o, gain) * _silu(g_ref[rows, vs].astype(F32))
            o_ref[rows, vs] = o.astype(BF16)


def _gla(q, k, la, v, g, gain, bsz, seq, *, n_chunk):
    hv = v.shape[1] // GLA_HEADS
    kw = q.shape[1]
    tc = n_chunk * CHUNK
    nt = seq // tc

    def spec(width):
        return pl.BlockSpec((tc, width), lambda b, c: (b * nt + c, 0))

    return pl.pallas_call(
        functools.partial(_gla_body, n_chunk=n_chunk, hv=hv),
        grid=(bsz, nt),
        in_specs=[spec(kw), spec(kw), spec(kw), spec(v.shape[1]), spec(v.shape[1]),
                  pl.BlockSpec((1, hv), lambda b, c: (0, 0))],
        out_specs=spec(v.shape[1]),
        out_shape=jax.ShapeDtypeStruct(v.shape, BF16),
        scratch_shapes=[pltpu.VMEM((GLA_HEADS, hv, GLA_HK_PAD), F32)],
        compiler_params=_params("parallel", "arbitrary"),
        name="gla",
    )(q, k, la, v, g, gain.reshape(1, hv))


def _fox_body(qi_tab, ki_tab, q_ref, k_ref, v_ref, kd_ref, ccol_ref, g_ref, o_ref,
              qa_scr, ka_scr, m_scr, l_scr, acc_scr, cc_scr, *, t, rq, hp):
    step = pl.program_id(2)
    head0 = pl.program_id(1) * hp
    qi = qi_tab[step]
    ki = ki_tab[step]

    @pl.when(ki == 0)
    def _():
        m_scr[...] = jnp.full_like(m_scr, -jnp.inf)
        l_scr[...] = jnp.zeros_like(l_scr)
        acc_scr[...] = jnp.zeros_like(acc_scr)
        blk = ccol_ref[...]
        lane = lax.broadcasted_iota(jnp.int32, blk.shape, 1)
        for j in range(hp):
            cc_scr[j] = jnp.sum(jnp.where(lane == head0 + j, blk, 0.0), axis=1, keepdims=True)
            qa_scr[j, :, :HEAD_DIM] = q_ref[:, j * HEAD_DIM:(j + 1) * HEAD_DIM]
            qa_scr[j, :, HEAD_DIM:] = (lane < DECAY_TERMS).astype(BF16)

    for j in range(hp):
        ka_scr[j, :, :HEAD_DIM] = k_ref[:, j * HEAD_DIM:(j + 1) * HEAD_DIM]
        ka_scr[j, :, HEAD_DIM:] = kd_ref[j]

    def update(j, r, n_cols, masked):
        rows = pl.ds(r * rq, rq)
        u = lax.dot_general(qa_scr[j, rows, :], ka_scr[j, :n_cols, :], (((1,), (1,)), ((), ())),
                            preferred_element_type=F32)
        if masked:
            row = lax.broadcasted_iota(jnp.int32, u.shape, 0)
            col = lax.broadcasted_iota(jnp.int32, u.shape, 1)
            u = jnp.where(col <= row + (n_cols - rq), u, -jnp.inf)
        cc = cc_scr[j, rows, :]
        m_prev = m_scr[j, rows, :]
        m_new = jnp.maximum(m_prev, jnp.max(u, axis=1, keepdims=True) + cc)
        p = jnp.exp2(u + (cc - m_new))
        alpha = jnp.exp2(m_prev - m_new)
        l_scr[j, rows, :] = alpha * l_scr[j, rows, :] + jnp.sum(p, axis=1, keepdims=True)
        acc_scr[j, rows, :] = alpha * acc_scr[j, rows, :] + jnp.dot(
            p.astype(BF16), v_ref[:n_cols, j * HEAD_DIM:(j + 1) * HEAD_DIM],
            preferred_element_type=F32)
        m_scr[j, rows, :] = m_new

    n_sub = t // rq

    @pl.when(ki < qi)
    def _():
        for r in range(n_sub):
            for j in range(hp):
                update(j, r, t, False)

    @pl.when(ki == qi)
    def _():
        for r in range(n_sub):
            for j in range(hp):
                update(j, r, (r + 1) * rq, True)
        for j in range(hp):
            hs = slice(j * HEAD_DIM, (j + 1) * HEAD_DIM)
            o = acc_scr[j] / l_scr[j]
            o_ref[:, hs] = (o * jax.nn.sigmoid(g_ref[:, hs].astype(F32))).astype(BF16)


def _fox(q, k, v, kd, c2, g, *, t, rq, hp):
    bsz, seq, width = q.shape
    heads = width // HEAD_DIM
    hw = hp * HEAD_DIM
    pairs = [(qi, ki) for qi in range(seq // t) for ki in range(qi + 1)]
    qi_tab = jnp.asarray(np.array([p[0] for p in pairs], np.int32))
    ki_tab = jnp.asarray(np.array([p[1] for p in pairs], np.int32))
    q_spec = pl.BlockSpec((None, t, hw), lambda b, h, s, qt, kt: (b, qt[s], h))
    kv_spec = pl.BlockSpec((None, t, hw), lambda b, h, s, qt, kt: (b, kt[s], h))
    grid_spec = pltpu.PrefetchScalarGridSpec(
        num_scalar_prefetch=2,
        grid=(bsz, heads // hp, len(pairs)),
        in_specs=[
            q_spec, kv_spec, kv_spec,
            pl.BlockSpec((None, hp, t, LANES), lambda b, h, s, qt, kt: (b, h, kt[s], 0)),
            pl.BlockSpec((None, t, LANES), lambda b, h, s, qt, kt: (b, qt[s], 0)),
            q_spec,
        ],
        out_specs=q_spec,
        scratch_shapes=[pltpu.VMEM((hp, t, 2 * HEAD_DIM), BF16),
                        pltpu.VMEM((hp, t, 2 * HEAD_DIM), BF16),
                        pltpu.VMEM((hp, t, 1), F32), pltpu.VMEM((hp, t, 1), F32),
                        pltpu.VMEM((hp, t, HEAD_DIM), F32), pltpu.VMEM((hp, t, 1), F32)],
    )
    return pl.pallas_call(
        functools.partial(_fox_body, t=t, rq=rq, hp=hp),
        grid_spec=grid_spec,
        out_shape=jax.ShapeDtypeStruct(q.shape, BF16),
        compiler_params=_params("parallel", "parallel", "arbitrary"),
        name="fox",
    )(qi_tab, ki_tab, q, k, v, kd, c2, g)


def _mem_attn_body(q_ref, mk_ref, mv_ref, o_ref):
    for h in range(MEM_HEADS):
        hs = slice(h * HEAD_DIM, (h + 1) * HEAD_DIM)
        s = lax.dot_general(q_ref[:, hs], mk_ref[:, hs], (((1,), (1,)), ((), ())),
                            preferred_element_type=F32)
        e = jnp.exp(s - jnp.max(s, axis=1, keepdims=True))
        p = e / jnp.sum(e, axis=1, keepdims=True)
        o_ref[:, hs] = jnp.dot(p.astype(BF16), mv_ref[:, hs],
                               preferred_element_type=F32).astype(BF16)


def _mem_attn(qm, mk, mv, bsz, seq, *, tm):
    n_mem = mk.shape[0] // bsz
    nt = seq // tm
    return pl.pallas_call(
        _mem_attn_body,
        grid=(bsz, nt),
        in_specs=[pl.BlockSpec((tm, MEM_W), lambda b, i: (b * nt + i, 0)),
                  pl.BlockSpec((n_mem, MEM_W), lambda b, i: (b, 0)),
                  pl.BlockSpec((n_mem, MEM_W), lambda b, i: (b, 0))],
        out_specs=pl.BlockSpec((tm, MEM_W), lambda b, i: (b * nt + i, 0)),
        out_shape=jax.ShapeDtypeStruct(qm.shape, BF16),
        compiler_params=_params("parallel", "arbitrary"),
        name="mem_attn",
    )(qm, mk, mv)


def _out_proj_body(x_ref, o_ref, mo_ref, wo_ref, wm_ref, y_ref):
    y = x_ref[...] + jnp.dot(o_ref[...], wo_ref[...], preferred_element_type=F32)
    y_ref[...] = y + jnp.dot(mo_ref[...], wm_ref[...], preferred_element_type=F32)


def _out_proj(x, o, mo, w_o, w_m, *, tm):
    n, d = x.shape
    return pl.pallas_call(
        _out_proj_body,
        grid=(n // tm,),
        in_specs=[pl.BlockSpec((tm, d), lambda i: (i, 0)),
                  pl.BlockSpec((tm, o.shape[1]), lambda i: (i, 0)),
                  pl.BlockSpec((tm, mo.shape[1]), lambda i: (i, 0)),
                  pl.BlockSpec(w_o.shape, lambda i: (0, 0)),
                  pl.BlockSpec(w_m.shape, lambda i: (0, 0))],
        out_specs=pl.BlockSpec((tm, d), lambda i: (i, 0)),
        out_shape=jax.ShapeDtypeStruct((n, d), F32),
        compiler_params=_params("parallel"),
        name="out_proj",
    )(x, o, mo, w_o, w_m)


def _pad_cols(w, width):
    return jnp.pad(w, ((0, 0), (0, width - w.shape[1])))


def _pad_gla_heads(w, hk):
    lead = w.shape[:-1]
    w = w.reshape(lead + (GLA_HEADS, hk))
    w = jnp.pad(w, [(0, 0)] * len(lead) + [(0, 0), (0, GLA_HK_PAD - hk)])
    return w.reshape(lead + (GLA_HEADS * GLA_HK_PAD,))


def _forward(x, mem, ffn_norm, ffn_w1, ffn_w3, ffn_w2, mix_norm, mem_norm, w_mem_kv,
             mem_q_norm, mem_k_norm, w_out, a_w_in, a_w_gate_up, a_b_gate, a_out_norm,
             b_w_in, b_q_norm, kv_norm, w_kv, b_f, k_norm, *, tiles):
    bsz, seq, d = x.shape
    n = bsz * seq
    depth = ffn_norm.shape[0]
    n_a = a_w_in.shape[0]
    main_w = d - MEM_W
    gla_key = main_w // 2
    hk = gla_key // GLA_HEADS
    fox_heads = main_w // HEAD_DIM
    tn = tiles["proj_tn"]
    attn_scale = HEAD_DIM ** -0.5

    w1 = ffn_w1.astype(BF16)
    w3 = ffn_w3.astype(BF16)
    w2 = ffn_w2.astype(BF16)
    x2 = x.reshape(n, d)
    mem2 = mem.reshape(bsz * mem.shape[1], d)

    def ffn(x2, l, s):
        return _ffn(x2, ffn_norm[l, s], w1[l, s], w3[l, s], w2[l, s],
                    tm=tiles["ffn_tm"], tf=tiles["ffn_tf"])

    k_sh = v_sh = c2 = k_decay = None
    for l in range(depth):
        if l == n_a:
            w = jnp.concatenate([w_kv[:, :2 * main_w], _pad_cols(w_kv[:, 2 * main_w:], tn)],
                                axis=1).astype(BF16)
            bias = _pad_cols(b_f.reshape(1, fox_heads), LANES)
            k_sh, v_sh, logf = _norm_proj(
                x2, kv_norm, w,
                [(main_w // tn, tn, BF16, _head_norm_ep(0, 1.0)),
                 (main_w // tn, tn, BF16, _plain_ep),
                 (1, LANES, F32, _forget_gate_ep(1))],
                [k_norm.reshape(1, HEAD_DIM), bias],
                tm=tiles["proj_tm"], tn=tn, name="kv_proj")
            c2, k_decay = _seq_cumsum(logf, bsz, seq, fox_heads, ts=tiles["cum_ts"])

        x2 = ffn(x2, l, 0)

        mk, mv = _norm_proj(
            mem2, mem_norm[l], w_mem_kv[l].astype(BF16),
            [(MEM_W // tn, tn, BF16, _head_norm_ep(0, 1.0)),
             (MEM_W // tn, tn, BF16, _plain_ep)],
            [mem_k_norm[l].reshape(1, HEAD_DIM)],
            tm=tiles["mem_tm"], tn=tn, name="mem_kv_proj")
        qm_ep = _head_norm_ep(0, attn_scale)

        if l < n_a:
            wi = a_w_in[l]
            c = np.cumsum([0, gla_key, gla_key, main_w, GLA_GATE_RANK, main_w, MEM_W])
            w = jnp.concatenate([
                _pad_gla_heads(wi[:, c[0]:c[1]], hk),
                _pad_gla_heads(wi[:, c[1]:c[2]], hk),
                wi[:, c[2]:c[3]],
                wi[:, c[4]:c[5]],
                wi[:, c[5]:c[6]],
                _pad_cols(wi[:, c[3]:c[4]], tn),
            ], axis=1).astype(BF16)
            kw = GLA_HEADS * GLA_HK_PAD
            w_up = jnp.pad(_pad_gla_heads(a_w_gate_up[l], hk),
                           ((0, tn - GLA_GATE_RANK), (0, 0))).astype(BF16)
            b_gate = _pad_gla_heads(a_b_gate[l].reshape(1, gla_key), hk)
            q, k, v, g, qm, la = _norm_proj(
                x2, mix_norm[l], w,
                [(kw // tn, tn, BF16, _scale_ep(hk ** -0.5)),
                 (kw // tn, tn, F32, _plain_ep),
                 (main_w // tn, tn, BF16, _plain_ep),
                 (main_w // tn, tn, BF16, _plain_ep),
                 (MEM_W // tn, tn, BF16, qm_ep),
                 (1, kw, F32, _gla_gate_ep(1, 2))],
                [mem_q_norm[l].reshape(1, HEAD_DIM), w_up, b_gate],
                tm=tiles["proj_tm"], tn=tn, name="gla_in_proj")
            o = _gla(q, k, la, v, g, a_out_norm[l], bsz, seq, n_chunk=tiles["gla_chunks"])
        else:
            jl = l - n_a
            q, g, qm = _norm_proj(
                x2, mix_norm[l], b_w_in[jl].astype(BF16),
                [(main_w // tn, tn, BF16, _head_norm_ep(0, attn_scale * LOG2E)),
                 (main_w // tn, tn, BF16, _plain_ep),
                 (MEM_W // tn, tn, BF16, _head_norm_ep(1, attn_scale))],
                [b_q_norm[jl].reshape(1, HEAD_DIM), mem_q_norm[l].reshape(1, HEAD_DIM)],
                tm=tiles["proj_tm"], tn=tn, name="fox_in_proj")
            o = _fox(q.reshape(bsz, seq, main_w), k_sh.reshape(bsz, seq, main_w),
                     v_sh.reshape(bsz, seq, main_w), k_decay, c2,
                     g.reshape(bsz, seq, main_w), t=tiles["fox_t"], rq=tiles["fox_rq"], hp=tiles["fox_hp"])
            o = o.reshape(n, main_w)

        mo = _mem_attn(qm, mk, mv, bsz, seq, tm=tiles["attn_tm"])
        wo = w_out[l].astype(BF16)
        x2 = _out_proj(x2, o, mo, wo[:main_w], wo[main_w:], tm=tiles["out_tm"])
        x2 = ffn(x2, l, 1)
    return x2.reshape(bsz, seq, d)


_TILES = dict(ffn_tm=512, ffn_tf=512, proj_tm=1024, proj_tn=512, mem_tm=512, cum_ts=512,
              gla_chunks=8, fox_t=1024, fox_rq=256, fox_hp=2, attn_tm=1024, out_tm=512)


def kernel(x, mem, ffn_norm, ffn_w1, ffn_w3, ffn_w2, mix_norm, mem_norm, w_mem_kv, mem_q_norm, mem_k_norm, w_out, a_w_in, a_w_gate_up, a_b_gate, a_out_norm, b_w_in, b_q_norm, kv_norm, w_kv, b_f, k_norm):
    return _forward(x, mem, ffn_norm, ffn_w1, ffn_w3, ffn_w2, mix_norm, mem_norm, w_mem_kv,
                    mem_q_norm, mem_k_norm, w_out, a_w_in, a_w_gate_up, a_b_gate, a_out_norm,
                    b_w_in, b_q_norm, kv_norm, w_kv, b_f, k_norm, tiles=_TILES)
```

```python
import functools

import numpy as np
import jax
import jax.numpy as jnp
from jax import lax
from jax.experimental import pallas as pl
from jax.experimental.pallas import tpu as pltpu

F32 = jnp.float32
BF16 = jnp.bfloat16

CHUNK = 64
HEAD_DIM = 128
MEM_HEADS = 4
MEM_W = MEM_HEADS * HEAD_DIM
GLA_HEADS = 4
GLA_GATE_RANK = 16
GLA_GATE_NORMALIZER = 16.0
EPS = 1e-6
LOG2E = 1.4426950408889634
DECAY_TERMS = 3

LANES = 128
V7X_VMEM_LIMIT_BYTES = 56 * 1024 * 1024

GLA_HK_PAD = 256


def _rms(x, gain):
    return x * lax.rsqrt(jnp.mean(x * x, axis=-1, keepdims=True) + EPS) * gain


def _log_sigmoid(x):
    return jnp.minimum(x, 0.0) - jnp.log1p(jnp.exp(-jnp.abs(x)))


def _silu(x):
    return x * jax.nn.sigmoid(x)


def _split3(x):
    hi = x.astype(BF16)
    r1 = x - hi.astype(F32)
    mid = r1.astype(BF16)
    lo = (r1 - mid.astype(F32)).astype(BF16)
    return hi, mid, lo


def _cumsum_rows(x):
    n = x.shape[0]
    row = lax.broadcasted_iota(jnp.int32, (n, n), 0)
    col = lax.broadcasted_iota(jnp.int32, (n, n), 1)
    tri = (row >= col).astype(BF16)
    return sum(jnp.dot(tri, term, preferred_element_type=F32) for term in _split3(x))


def _params(*semantics):
    return pltpu.CompilerParams(dimension_semantics=semantics,
                                vmem_limit_bytes=V7X_VMEM_LIMIT_BYTES)


def _ffn_body(x_ref, g_ref, w1_ref, w3_ref, w2_ref, o_ref, h_scr):
    def ff_tile(h):
        a = jnp.dot(h, w1_ref[...], preferred_element_type=F32)
        b = jnp.dot(h, w3_ref[...], preferred_element_type=F32)
        act = (_silu(a) * b * 0.5).astype(BF16)
        return jnp.dot(act, w2_ref[...], preferred_element_type=F32)

    @pl.when(pl.program_id(1) == 0)
    def _():
        x = x_ref[...]
        h = _rms(x, g_ref[...]).astype(BF16)
        h_scr[...] = h
        o_ref[...] = x + ff_tile(h)

    @pl.when(pl.program_id(1) > 0)
    def _():
        o_ref[...] += ff_tile(h_scr[...])


def _ffn(x, gain, w1, w3, w2, layer, half, *, tm, tf):
    n, d = x.shape
    f = w1.shape[-1]
    return pl.pallas_call(
        _ffn_body,
        grid=(n // tm, f // tf),
        in_specs=[
            pl.BlockSpec((tm, d), lambda i, j: (i, 0)),
            pl.BlockSpec((1, d), lambda i, j: (0, 0)),
            pl.BlockSpec((None, None, d, tf), lambda i, j: (layer, half, 0, j)),
            pl.BlockSpec((None, None, d, tf), lambda i, j: (layer, half, 0, j)),
            pl.BlockSpec((None, None, tf, d), lambda i, j: (layer, half, j, 0)),
        ],
        out_specs=pl.BlockSpec((tm, d), lambda i, j: (i, 0)),
        out_shape=jax.ShapeDtypeStruct((n, d), F32),
        scratch_shapes=[pltpu.VMEM((tm, d), BF16)],
        compiler_params=_params("parallel", "arbitrary"),
        name="ffn",
    )(x, gain.reshape(1, d), w1, w3, w2)


def _norm_proj_body(*refs, groups, n_extra):
    x_ref, g_ref, w_ref = refs[:3]
    extra = refs[3:3 + n_extra]
    outs = refs[3 + n_extra:3 + n_extra + len(groups)]
    h_scr = refs[-1]
    j = pl.program_id(1)

    def tile(h, fn, o_ref):
        acc = jnp.dot(h, w_ref[...], preferred_element_type=F32)
        o_ref[...] = fn(acc, extra).astype(o_ref.dtype)

    @pl.when(j == 0)
    def _():
        h = _rms(x_ref[...], g_ref[...]).astype(BF16)
        h_scr[...] = h
        tile(h, groups[0][3], outs[0])

    start = 0
    for (n_tiles, _, _, fn), o_ref in zip(groups, outs):
        @pl.when((j >= max(start, 1)) & (j < start + n_tiles))
        def _(fn=fn, o_ref=o_ref):
            tile(h_scr[...], fn, o_ref)
        start += n_tiles


def _norm_proj(x, gain, w, groups, extras, *, tm, tn, name):
    n, d = x.shape
    assert w.shape[1] == tn * sum(g[0] for g in groups)
    in_specs = [
        pl.BlockSpec((tm, d), lambda i, j: (i, 0)),
        pl.BlockSpec((1, d), lambda i, j: (0, 0)),
        pl.BlockSpec((d, tn), lambda i, j: (0, j)),
    ] + [pl.BlockSpec(e.shape, lambda i, j: (0, 0)) for e in extras]
    out_specs, out_shapes = [], []
    start = 0
    for n_tiles, width, dtype, _ in groups:
        out_specs.append(pl.BlockSpec(
            (tm, width),
            lambda i, j, s=start, c=n_tiles: (i, jnp.clip(j - s, 0, c - 1))))
        out_shapes.append(jax.ShapeDtypeStruct((n, width * n_tiles), dtype))
        start += n_tiles
    return pl.pallas_call(
        functools.partial(_norm_proj_body, groups=groups, n_extra=len(extras)),
        grid=(n // tm, start),
        in_specs=in_specs,
        out_specs=out_specs,
        out_shape=out_shapes,
        scratch_shapes=[pltpu.VMEM((tm, d), BF16)],
        compiler_params=_params("parallel", "arbitrary"),
        name=name,
    )(x, gain.reshape(1, d), w, *extras)


def _scale_ep(scale):
    return lambda acc, extra: acc * scale


def _plain_ep(acc, extra):
    return acc


def _head_norm_ep(gain_idx, scale):
    def fn(acc, extra):
        gain = extra[gain_idx][...]
        parts = []
        for h in range(acc.shape[1] // HEAD_DIM):
            a = acc[:, h * HEAD_DIM:(h + 1) * HEAD_DIM]
            parts.append(_rms(a, gain) * scale)
        return jnp.concatenate(parts, axis=-1)
    return fn


def _gla_gate_ep(w_up_idx, bias_idx):
    def fn(acc, extra):
        z = jnp.dot(acc.astype(BF16), extra[w_up_idx][...], preferred_element_type=F32)
        return _log_sigmoid(z + extra[bias_idx][...]) / GLA_GATE_NORMALIZER
    return fn


def _forget_gate_ep(bias_idx):
    def fn(acc, extra):
        return _log_sigmoid(acc[:, :LANES] + extra[bias_idx][...])
    return fn


def _cumsum_body(lf_ref, col_ref, kd_ref, carry):
    @pl.when(pl.program_id(1) == 0)
    def _():
        carry[...] = jnp.zeros_like(carry)

    cum = _cumsum_rows(lf_ref[...]) + carry[...]
    carry[...] = cum[cum.shape[0] - 1:, :]
    c2 = cum * LOG2E
    col_ref[...] = c2
    terms = [term.astype(F32) for term in _split3(-c2)]
    lane = lax.broadcasted_iota(jnp.int32, c2.shape, 1)
    for h in range(kd_ref.shape[0]):
        tile = jnp.zeros_like(c2)
        for i, term in enumerate(terms):
            tile = jnp.where(lane == i, term[:, h:h + 1], tile)
        kd_ref[h] = tile.astype(BF16)


def _seq_cumsum(logf, bsz, seq, heads, *, ts):
    return pl.pallas_call(
        _cumsum_body,
        grid=(bsz, seq // ts),
        in_specs=[pl.BlockSpec((None, ts, LANES), lambda b, s: (b, s, 0))],
        out_specs=[pl.BlockSpec((None, ts, LANES), lambda b, s: (b, s, 0)),
                   pl.BlockSpec((None, heads, ts, LANES), lambda b, s: (b, 0, s, 0))],
        out_shape=[jax.ShapeDtypeStruct((bsz, seq, LANES), F32),
                   jax.ShapeDtypeStruct((bsz, heads, seq, LANES), BF16)],
        scratch_shapes=[pltpu.VMEM((1, LANES), F32)],
        compiler_params=_params("parallel", "arbitrary"),
        name="forget_cumsum",
    )(logf.reshape(bsz, seq, LANES))


def _gla_body(q_ref, k_ref, la_ref, v_ref, g_ref, gain_ref, o_ref, st_ref, *, n_chunk, hv):
    @pl.when(pl.program_id(1) == 0)
    def _():
        st_ref[...] = jnp.zeros_like(st_ref)

    gain = gain_ref[...]
    for c in range(n_chunk):
        rows = pl.ds(c * CHUNK, CHUNK)
        cum = _cumsum_rows(la_ref[rows, :])
        total = cum[CHUNK - 1:, :]
        k_dec = (k_ref[rows, :] * jnp.exp(total - cum)).astype(BF16)
        a_chunk = jnp.exp(total)
        for h in range(GLA_HEADS):
            ks = slice(h * GLA_HK_PAD, (h + 1) * GLA_HK_PAD)
            vs = slice(h * hv, (h + 1) * hv)
            upd = lax.dot_general(v_ref[rows, vs], k_dec[:, ks],
                                  (((0,), (0,)), ((), ())), preferred_element_type=F32)
            st = a_chunk[:, ks] * st_ref[h] + upd
            st_ref[h] = st
            o = lax.dot_general(q_ref[rows, ks], st.astype(BF16),
                                (((1,), (1,)), ((), ())), preferred_element_type=F32)
            o = _rms(o, gain) * _silu(g_ref[rows, vs].astype(F32))
            o_ref[rows, vs] = o.astype(BF16)


def _gla(q, k, la, v, g, gain, bsz, seq, *, n_chunk):
    hv = v.shape[1] // GLA_HEADS
    kw = q.shape[1]
    tc = n_chunk * CHUNK
    nt = seq // tc

    def spec(width):
        return pl.BlockSpec((tc, width), lambda b, c: (b * nt + c, 0))

    return pl.pallas_call(
        functools.partial(_gla_body, n_chunk=n_chunk, hv=hv),
        grid=(bsz, nt),
        in_specs=[spec(kw), spec(kw), spec(kw), spec(v.shape[1]), spec(v.shape[1]),
                  pl.BlockSpec((1, hv), lambda b, c: (0, 0))],
        out_specs=spec(v.shape[1]),
        out_shape=jax.ShapeDtypeStruct(v.shape, BF16),
        scratch_shapes=[pltpu.VMEM((GLA_HEADS, hv, GLA_HK_PAD), F32)],
        compiler_params=_params("parallel", "arbitrary"),
        name="gla",
    )(q, k, la, v, g, gain.reshape(1, hv))


def _fox_body(qi_tab, ki_tab, q_ref, k_ref, v_ref, kd_ref, ccol_ref, g_ref, o_ref,
              qa_scr, ka_scr, m_scr, l_scr, acc_scr, cc_scr, *, t, rq, hp):
    step = pl.program_id(2)
    head0 = pl.program_id(1) * hp
    qi = qi_tab[step]
    ki = ki_tab[step]

    @pl.when(ki == 0)
    def _():
        m_scr[...] = jnp.full_like(m_scr, -jnp.inf)
        l_scr[...] = jnp.zeros_like(l_scr)
        acc_scr[...] = jnp.zeros_like(acc_scr)
        blk = ccol_ref[...]
        lane = lax.broadcasted_iota(jnp.int32, blk.shape, 1)
        for j in range(hp):
            cc_scr[j] = jnp.sum(jnp.where(lane == head0 + j, blk, 0.0), axis=1, keepdims=True)
            qa_scr[j, :, :HEAD_DIM] = q_ref[:, j * HEAD_DIM:(j + 1) * HEAD_DIM]
            qa_scr[j, :, HEAD_DIM:] = (lane < DECAY_TERMS).astype(BF16)

    for j in range(hp):
        ka_scr[j, :, :HEAD_DIM] = k_ref[:, j * HEAD_DIM:(j + 1) * HEAD_DIM]
        ka_scr[j, :, HEAD_DIM:] = kd_ref[j]

    def update(j, r, n_cols, masked):
        rows = pl.ds(r * rq, rq)
        u = lax.dot_general(qa_scr[j, rows, :], ka_scr[j, :n_cols, :], (((1,), (1,)), ((), ())),
                            preferred_element_type=F32)
        if masked:
            row = lax.broadcasted_iota(jnp.int32, u.shape, 0)
            col = lax.broadcasted_iota(jnp.int32, u.shape, 1)
            u = jnp.where(col <= row + (n_cols - rq), u, -jnp.inf)
        cc = cc_scr[j, rows, :]
        m_prev = m_scr[j, rows, :]
        m_new = jnp.maximum(m_prev, jnp.max(u, axis=1, keepdims=True) + cc)
        p = jnp.exp2(u + (cc - m_new))
        alpha = jnp.exp2(m_prev - m_new)
        l_scr[j, rows, :] = alpha * l_scr[j, rows, :] + jnp.sum(p, axis=1, keepdims=True)
        acc_scr[j, rows, :] = alpha * acc_scr[j, rows, :] + jnp.dot(
            p.astype(BF16), v_ref[:n_cols, j * HEAD_DIM:(j + 1) * HEAD_DIM],
            preferred_element_type=F32)
        m_scr[j, rows, :] = m_new

    n_sub = t // rq

    @pl.when(ki < qi)
    def _():
        for r in range(n_sub):
            for j in range(hp):
                update(j, r, t, False)

    @pl.when(ki == qi)
    def _():
        for r in range(n_sub):
            for j in range(hp):
                update(j, r, (r + 1) * rq, True)
        for j in range(hp):
            hs = slice(j * HEAD_DIM, (j + 1) * HEAD_DIM)
            o = acc_scr[j] / l_scr[j]
            o_ref[:, hs] = (o * jax.nn.sigmoid(g_ref[:, hs].astype(F32))).astype(BF16)


def _fox(q, k, v, kd, c2, g, *, t, rq, hp):
    bsz, seq, width = q.shape
    heads = width // HEAD_DIM
    hw = hp * HEAD_DIM
    pairs = [(qi, ki) for qi in range(seq // t) for ki in range(qi + 1)]
    qi_tab = jnp.asarray(np.array([p[0] for p in pairs], np.int32))
    ki_tab = jnp.asarray(np.array([p[1] for p in pairs], np.int32))
    q_spec = pl.BlockSpec((None, t, hw), lambda b, h, s, qt, kt: (b, qt[s], h))
    kv_spec = pl.BlockSpec((None, t, hw), lambda b, h, s, qt, kt: (b, kt[s], h))
    grid_spec = pltpu.PrefetchScalarGridSpec(
        num_scalar_prefetch=2,
        grid=(bsz, heads // hp, len(pairs)),
        in_specs=[
            q_spec, kv_spec, kv_spec,
            pl.BlockSpec((None, hp, t, LANES), lambda b, h, s, qt, kt: (b, h, kt[s], 0)),
            pl.BlockSpec((None, t, LANES), lambda b, h, s, qt, kt: (b, qt[s], 0)),
            q_spec,
        ],
        out_specs=q_spec,
        scratch_shapes=[pltpu.VMEM((hp, t, 2 * HEAD_DIM), BF16),
                        pltpu.VMEM((hp, t, 2 * HEAD_DIM), BF16),
                        pltpu.VMEM((hp, t, 1), F32), pltpu.VMEM((hp, t, 1), F32),
                        pltpu.VMEM((hp, t, HEAD_DIM), F32), pltpu.VMEM((hp, t, 1), F32)],
    )
    return pl.pallas_call(
        functools.partial(_fox_body, t=t, rq=rq, hp=hp),
        grid_spec=grid_spec,
        out_shape=jax.ShapeDtypeStruct(q.shape, BF16),
        compiler_params=_params("parallel", "parallel", "arbitrary"),
        name="fox",
    )(qi_tab, ki_tab, q, k, v, kd, c2, g)


def _mem_attn_body(q_ref, mk_ref, mv_ref, o_ref):
    for h in range(MEM_HEADS):
        hs = slice(h * HEAD_DIM, (h + 1) * HEAD_DIM)
        s = lax.dot_general(q_ref[:, hs], mk_ref[:, hs], (((1,), (1,)), ((), ())),
                            preferred_element_type=F32)
        e = jnp.exp(s - jnp.max(s, axis=1, keepdims=True))
        p = e / jnp.sum(e, axis=1, keepdims=True)
        o_ref[:, hs] = jnp.dot(p.astype(BF16), mv_ref[:, hs],
                               preferred_element_type=F32).astype(BF16)


def _mem_attn(qm, mk, mv, bsz, seq, *, tm):
    n_mem = mk.shape[0] // bsz
    nt = seq // tm
    return pl.pallas_call(
        _mem_attn_body,
        grid=(bsz, nt),
        in_specs=[pl.BlockSpec((tm, MEM_W), lambda b, i: (b * nt + i, 0)),
                  pl.BlockSpec((n_mem, MEM_W), lambda b, i: (b, 0)),
                  pl.BlockSpec((n_mem, MEM_W), lambda b, i: (b, 0))],
        out_specs=pl.BlockSpec((tm, MEM_W), lambda b, i: (b * nt + i, 0)),
        out_shape=jax.ShapeDtypeStruct(qm.shape, BF16),
        compiler_params=_params("parallel", "arbitrary"),
        name="mem_attn",
    )(qm, mk, mv)


def _out_proj_body(x_ref, o_ref, mo_ref, wo_ref, wm_ref, y_ref):
    y = x_ref[...] + jnp.dot(o_ref[...], wo_ref[...], preferred_element_type=F32)
    y_ref[...] = y + jnp.dot(mo_ref[...], wm_ref[...], preferred_element_type=F32)


def _out_proj(x, o, mo, w_o, w_m, *, tm):
    n, d = x.shape
    return pl.pallas_call(
        _out_proj_body,
        grid=(n // tm,),
        in_specs=[pl.BlockSpec((tm, d), lambda i: (i, 0)),
                  pl.BlockSpec((tm, o.shape[1]), lambda i: (i, 0)),
                  pl.BlockSpec((tm, mo.shape[1]), lambda i: (i, 0)),
                  pl.BlockSpec(w_o.shape, lambda i: (0, 0)),
                  pl.BlockSpec(w_m.shape, lambda i: (0, 0))],
        out_specs=pl.BlockSpec((tm, d), lambda i: (i, 0)),
        out_shape=jax.ShapeDtypeStruct((n, d), F32),
        compiler_params=_params("parallel"),
        name="out_proj",
    )(x, o, mo, w_o, w_m)


def _pad_cols(w, width):
    return jnp.pad(w, ((0, 0), (0, width - w.shape[1])))


def _pad_gla_heads(w, hk):
    lead = w.shape[:-1]
    w = w.reshape(lead + (GLA_HEADS, hk))
    w = jnp.pad(w, [(0, 0)] * len(lead) + [(0, 0), (0, GLA_HK_PAD - hk)])
    return w.reshape(lead + (GLA_HEADS * GLA_HK_PAD,))


def _forward(x, mem, ffn_norm, ffn_w1, ffn_w3, ffn_w2, mix_norm, mem_norm, w_mem_kv,
             mem_q_norm, mem_k_norm, w_out, a_w_in, a_w_gate_up, a_b_gate, a_out_norm,
             b_w_in, b_q_norm, kv_norm, w_kv, b_f, k_norm, *, tiles):
    bsz, seq, d = x.shape
    n = bsz * seq
    depth = ffn_norm.shape[0]
    n_a = a_w_in.shape[0]
    main_w = d - MEM_W
    gla_key = main_w // 2
    hk = gla_key // GLA_HEADS
    fox_heads = main_w // HEAD_DIM
    tn = tiles["proj_tn"]
    attn_scale = HEAD_DIM ** -0.5

    w1 = ffn_w1.astype(BF16)
    w3 = ffn_w3.astype(BF16)
    w2 = ffn_w2.astype(BF16)
    x2 = x.reshape(n, d)
    mem2 = mem.reshape(bsz * mem.shape[1], d)

    def ffn(x2, l, s):
        return _ffn(x2, ffn_norm[l, s], w1, w3, w2, l, s,
                    tm=tiles["ffn_tm"], tf=tiles["ffn_tf"])

    k_sh = v_sh = c2 = k_decay = None
    for l in range(depth):
        if l == n_a:
            w = jnp.concatenate([w_kv[:, :2 * main_w], _pad_cols(w_kv[:, 2 * main_w:], tn)],
                                axis=1).astype(BF16)
            bias = _pad_cols(b_f.reshape(1, fox_heads), LANES)
            k_sh, v_sh, logf = _norm_proj(
                x2, kv_norm, w,
                [(main_w // tn, tn, BF16, _head_norm_ep(0, 1.0)),
                 (main_w // tn, tn, BF16, _plain_ep),
                 (1, LANES, F32, _forget_gate_ep(1))],
                [k_norm.reshape(1, HEAD_DIM), bias],
                tm=tiles["proj_tm"], tn=tn, name="kv_proj")
            c2, k_decay = _seq_cumsum(logf, bsz, seq, fox_heads, ts=tiles["cum_ts"])

        x2 = ffn(x2, l, 0)

        mk, mv = _norm_proj(
            mem2, mem_norm[l], w_mem_kv[l].astype(BF16),
            [(MEM_W // tn, tn, BF16, _head_norm_ep(0, 1.0)),
             (MEM_W // tn, tn, BF16, _plain_ep)],
            [mem_k_norm[l].reshape(1, HEAD_DIM)],
            tm=tiles["mem_tm"], tn=tn, name="mem_kv_proj")
        qm_ep = _head_norm_ep(0, attn_scale)

        if l < n_a:
            wi = a_w_in[l]
            c = np.cumsum([0, gla_key, gla_key, main_w, GLA_GATE_RANK, main_w, MEM_W])
            w = jnp.concatenate([
                _pad_gla_heads(wi[:, c[0]:c[1]], hk),
                _pad_gla_heads(wi[:, c[1]:c[2]], hk),
                wi[:, c[2]:c[3]],
                wi[:, c[4]:c[5]],
                wi[:, c[5]:c[6]],
                _pad_cols(wi[:, c[3]:c[4]], tn),
            ], axis=1).astype(BF16)
            kw = GLA_HEADS * GLA_HK_PAD
            w_up = jnp.pad(_pad_gla_heads(a_w_gate_up[l], hk),
                           ((0, tn - GLA_GATE_RANK), (0, 0))).astype(BF16)
            b_gate = _pad_gla_heads(a_b_gate[l].reshape(1, gla_key), hk)
            q, k, v, g, qm, la = _norm_proj(
                x2, mix_norm[l], w,
                [(kw // tn, tn, BF16, _scale_ep(hk ** -0.5)),
                 (kw // tn, tn, F32, _plain_ep),
                 (main_w // tn, tn, BF16, _plain_ep),
                 (main_w // tn, tn, BF16, _plain_ep),
                 (MEM_W // tn, tn, BF16, qm_ep),
                 (1, kw, F32, _gla_gate_ep(1, 2))],
                [mem_q_norm[l].reshape(1, HEAD_DIM), w_up, b_gate],
                tm=tiles["proj_tm"], tn=tn, name="gla_in_proj")
            o = _gla(q, k, la, v, g, a_out_norm[l], bsz, seq, n_chunk=tiles["gla_chunks"])
        else:
            jl = l - n_a
            q, g, qm = _norm_proj(
                x2, mix_norm[l], b_w_in[jl].astype(BF16),
                [(main_w // tn, tn, BF16, _head_norm_ep(0, attn_scale * LOG2E)),
                 (main_w // tn, tn, BF16, _plain_ep),
                 (MEM_W // tn, tn, BF16, _head_norm_ep(1, attn_scale))],
                [b_q_norm[jl].reshape(1, HEAD_DIM), mem_q_norm[l].reshape(1, HEAD_DIM)],
                tm=tiles["proj_tm"], tn=tn, name="fox_in_proj")
            o = _fox(q.reshape(bsz, seq, main_w), k_sh.reshape(bsz, seq, main_w),
                     v_sh.reshape(bsz, seq, main_w), k_decay, c2,
                     g.reshape(bsz, seq, main_w), t=tiles["fox_t"], rq=tiles["fox_rq"], hp=tiles["fox_hp"])
            o = o.reshape(n, main_w)

        mo = _mem_attn(qm, mk, mv, bsz, seq, tm=tiles["attn_tm"])
        wo = w_out[l].astype(BF16)
        x2 = _out_proj(x2, o, mo, wo[:main_w], wo[main_w:], tm=tiles["out_tm"])
        x2 = ffn(x2, l, 1)
    return x2.reshape(bsz, seq, d)


_TILES = dict(ffn_tm=512, ffn_tf=512, proj_tm=1024, proj_tn=512, mem_tm=512, cum_ts=512,
              gla_chunks=8, fox_t=1024, fox_rq=256, fox_hp=2, attn_tm=1024, out_tm=512)


def kernel(x, mem, ffn_norm, ffn_w1, ffn_w3, ffn_w2, mix_norm, mem_norm, w_mem_kv, mem_q_norm, mem_k_norm, w_out, a_w_in, a_w_gate_up, a_b_gate, a_out_norm, b_w_in, b_q_norm, kv_norm, w_kv, b_f, k_norm):
    return _forward(x, mem, ffn_norm, ffn_w1, ffn_w3, ffn_w2, mix_norm, mem_norm, w_mem_kv,
                    mem_q_norm, mem_k_norm, w_out, a_w_in, a_w_gate_up, a_b_gate, a_out_norm,
                    b_w_in, b_q_norm, kv_norm, w_kv, b_f, k_norm, tiles=_TILES)
```

```python
import functools

import numpy as np
import jax
import jax.numpy as jnp
from jax import lax
from jax.experimental import pallas as pl
from jax.experimental.pallas import tpu as pltpu

F32 = jnp.float32
BF16 = jnp.bfloat16

CHUNK = 64
HEAD_DIM = 128
MEM_HEADS = 4
MEM_W = MEM_HEADS * HEAD_DIM
GLA_HEADS = 4
GLA_GATE_RANK = 16
GLA_GATE_NORMALIZER = 16.0
EPS = 1e-6
LOG2E = 1.4426950408889634
DECAY_TERMS = 3

LANES = 128
V7X_VMEM_LIMIT_BYTES = 56 * 1024 * 1024

GLA_HK_PAD = 256


def _rms(x, gain):
    return x * lax.rsqrt(jnp.mean(x * x, axis=-1, keepdims=True) + EPS) * gain


def _log_sigmoid(x):
    return jnp.minimum(x, 0.0) - jnp.log1p(jnp.exp(-jnp.abs(x)))


def _silu(x):
    return x * jax.nn.sigmoid(x)


def _split3(x):
    hi = x.astype(BF16)
    r1 = x - hi.astype(F32)
    mid = r1.astype(BF16)
    lo = (r1 - mid.astype(F32)).astype(BF16)
    return hi, mid, lo


def _cumsum_rows(x):
    n = x.shape[0]
    row = lax.broadcasted_iota(jnp.int32, (n, n), 0)
    col = lax.broadcasted_iota(jnp.int32, (n, n), 1)
    tri = (row >= col).astype(BF16)
    return sum(jnp.dot(tri, term, preferred_element_type=F32) for term in _split3(x))


def _params(*semantics):
    return pltpu.CompilerParams(dimension_semantics=semantics,
                                vmem_limit_bytes=V7X_VMEM_LIMIT_BYTES)


def _ffn_body(x_ref, g_ref, w1_ref, w3_ref, w2_ref, o_ref, h_scr):
    def ff_tile(h):
        a = jnp.dot(h, w1_ref[...], preferred_element_type=F32)
        b = jnp.dot(h, w3_ref[...], preferred_element_type=F32)
        act = (_silu(a) * b * 0.5).astype(BF16)
        return jnp.dot(act, w2_ref[...], preferred_element_type=F32)

    @pl.when(pl.program_id(1) == 0)
    def _():
        x = x_ref[...]
        h = _rms(x, g_ref[...]).astype(BF16)
        h_scr[...] = h
        o_ref[...] = x + ff_tile(h)

    @pl.when(pl.program_id(1) > 0)
    def _():
        o_ref[...] += ff_tile(h_scr[...])


def _ffn(x, gain, w1, w3, w2, layer, half, *, tm, tf):
    n, d = x.shape
    f = w1.shape[-1]
    return pl.pallas_call(
        _ffn_body,
        grid=(n // tm, f // tf),
        in_specs=[
            pl.BlockSpec((tm, d), lambda i, j: (i, 0)),
            pl.BlockSpec((1, d), lambda i, j: (0, 0)),
            pl.BlockSpec((None, None, d, tf), lambda i, j: (layer, half, 0, j)),
            pl.BlockSpec((None, None, d, tf), lambda i, j: (layer, half, 0, j)),
            pl.BlockSpec((None, None, tf, d), lambda i, j: (layer, half, j, 0)),
        ],
        out_specs=pl.BlockSpec((tm, d), lambda i, j: (i, 0)),
        out_shape=jax.ShapeDtypeStruct((n, d), F32),
        scratch_shapes=[pltpu.VMEM((tm, d), BF16)],
        compiler_params=_params("parallel", "arbitrary"),
        name="ffn",
    )(x, gain.reshape(1, d), w1, w3, w2)


def _norm_proj_body(*refs, groups, n_extra):
    x_ref, g_ref, w_ref = refs[:3]
    extra = refs[3:3 + n_extra]
    outs = refs[3 + n_extra:3 + n_extra + len(groups)]
    h_scr = refs[-1]
    j = pl.program_id(1)

    def tile(h, fn, o_ref):
        acc = jnp.dot(h, w_ref[...], preferred_element_type=F32)
        o_ref[...] = fn(acc, extra).astype(o_ref.dtype)

    @pl.when(j == 0)
    def _():
        h = _rms(x_ref[...], g_ref[...]).astype(BF16)
        h_scr[...] = h
        tile(h, groups[0][3], outs[0])

    start = 0
    for (n_tiles, _, _, fn), o_ref in zip(groups, outs):
        @pl.when((j >= max(start, 1)) & (j < start + n_tiles))
        def _(fn=fn, o_ref=o_ref):
            tile(h_scr[...], fn, o_ref)
        start += n_tiles


def _norm_proj(x, gain, w, groups, extras, *, tm, tn, name):
    n, d = x.shape
    assert w.shape[1] == tn * sum(g[0] for g in groups)
    in_specs = [
        pl.BlockSpec((tm, d), lambda i, j: (i, 0)),
        pl.BlockSpec((1, d), lambda i, j: (0, 0)),
        pl.BlockSpec((d, tn), lambda i, j: (0, j)),
    ] + [pl.BlockSpec(e.shape, lambda i, j: (0, 0)) for e in extras]
    out_specs, out_shapes = [], []
    start = 0
    for n_tiles, width, dtype, _ in groups:
        out_specs.append(pl.BlockSpec(
            (tm, width),
            lambda i, j, s=start, c=n_tiles: (i, jnp.clip(j - s, 0, c - 1))))
        out_shapes.append(jax.ShapeDtypeStruct((n, width * n_tiles), dtype))
        start += n_tiles
    return pl.pallas_call(
        functools.partial(_norm_proj_body, groups=groups, n_extra=len(extras)),
        grid=(n // tm, start),
        in_specs=in_specs,
        out_specs=out_specs,
        out_shape=out_shapes,
        scratch_shapes=[pltpu.VMEM((tm, d), BF16)],
        compiler_params=_params("parallel", "arbitrary"),
        name=name,
    )(x, gain.reshape(1, d), w, *extras)


def _scale_ep(scale):
    return lambda acc, extra: acc * scale


def _plain_ep(acc, extra):
    return acc


def _head_norm_ep(gain_idx, scale):
    def fn(acc, extra):
        gain = extra[gain_idx][...]
        parts = []
        for h in range(acc.shape[1] // HEAD_DIM):
            a = acc[:, h * HEAD_DIM:(h + 1) * HEAD_DIM]
            parts.append(_rms(a, gain) * scale)
        return jnp.concatenate(parts, axis=-1)
    return fn


def _gla_gate_ep(w_up_idx, bias_idx):
    def fn(acc, extra):
        z = jnp.dot(acc.astype(BF16), extra[w_up_idx][...], preferred_element_type=F32)
        return _log_sigmoid(z + extra[bias_idx][...]) / GLA_GATE_NORMALIZER
    return fn


def _forget_gate_ep(bias_idx):
    def fn(acc, extra):
        return _log_sigmoid(acc[:, :LANES] + extra[bias_idx][...])
    return fn


def _cumsum_body(lf_ref, col_ref, kd_ref, carry):
    @pl.when(pl.program_id(1) == 0)
    def _():
        carry[...] = jnp.zeros_like(carry)

    cum = _cumsum_rows(lf_ref[...]) + carry[...]
    carry[...] = cum[cum.shape[0] - 1:, :]
    c2 = cum * LOG2E
    col_ref[...] = c2
    terms = [term.astype(F32) for term in _split3(-c2)]
    lane = lax.broadcasted_iota(jnp.int32, c2.shape, 1)
    for h in range(kd_ref.shape[0]):
        tile = jnp.zeros_like(c2)
        for i, term in enumerate(terms):
            tile = jnp.where(lane == i, term[:, h:h + 1], tile)
        kd_ref[h] = tile.astype(BF16)


def _seq_cumsum(logf, bsz, seq, heads, *, ts):
    return pl.pallas_call(
        _cumsum_body,
        grid=(bsz, seq // ts),
        in_specs=[pl.BlockSpec((None, ts, LANES), lambda b, s: (b, s, 0))],
        out_specs=[pl.BlockSpec((None, ts, LANES), lambda b, s: (b, s, 0)),
                   pl.BlockSpec((None, heads, ts, LANES), lambda b, s: (b, 0, s, 0))],
        out_shape=[jax.ShapeDtypeStruct((bsz, seq, LANES), F32),
                   jax.ShapeDtypeStruct((bsz, heads, seq, LANES), BF16)],
        scratch_shapes=[pltpu.VMEM((1, LANES), F32)],
        compiler_params=_params("parallel", "arbitrary"),
        name="forget_cumsum",
    )(logf.reshape(bsz, seq, LANES))


def _gla_body(q_ref, k_ref, la_ref, v_ref, g_ref, gain_ref, o_ref, st_ref, *, n_chunk, hv):
    @pl.when(pl.program_id(1) == 0)
    def _():
        st_ref[...] = jnp.zeros_like(st_ref)

    gain = gain_ref[...]
    for c in range(n_chunk):
        rows = pl.ds(c * CHUNK, CHUNK)
        cum = _cumsum_rows(la_ref[rows, :])
        total = cum[CHUNK - 1:, :]
        k_dec = (k_ref[rows, :] * jnp.exp(total - cum)).astype(BF16)
        a_chunk = jnp.exp(total)
        for h in range(GLA_HEADS):
            ks = slice(h * GLA_HK_PAD, (h + 1) * GLA_HK_PAD)
            vs = slice(h * hv, (h + 1) * hv)
            upd = lax.dot_general(v_ref[rows, vs], k_dec[:, ks],
                                  (((0,), (0,)), ((), ())), preferred_element_type=F32)
            st = a_chunk[:, ks] * st_ref[h] + upd
            st_ref[h] = st
            o = lax.dot_general(q_ref[rows, ks], st.astype(BF16),
                                (((1,), (1,)), ((), ())), preferred_element_type=F32)
            o = _rms(o, gain) * _silu(g_ref[rows, vs].astype(F32))
            o_ref[rows, vs] = o.astype(BF16)


def _gla(q, k, la, v, g, gain, bsz, seq, *, n_chunk):
    hv = v.shape[1] // GLA_HEADS
    kw = q.shape[1]
    tc = n_chunk * CHUNK
    nt = seq // tc

    def spec(width):
        return pl.BlockSpec((tc, width), lambda b, c: (b * nt + c, 0))

    return pl.pallas_call(
        functools.partial(_gla_body, n_chunk=n_chunk, hv=hv),
        grid=(bsz, nt),
        in_specs=[spec(kw), spec(kw), spec(kw), spec(v.shape[1]), spec(v.shape[1]),
                  pl.BlockSpec((1, hv), lambda b, c: (0, 0))],
        out_specs=spec(v.shape[1]),
        out_shape=jax.ShapeDtypeStruct(v.shape, BF16),
        scratch_shapes=[pltpu.VMEM((GLA_HEADS, hv, GLA_HK_PAD), F32)],
        compiler_params=_params("parallel", "arbitrary"),
        name="gla",
    )(q, k, la, v, g, gain.reshape(1, hv))


def _fox_body(qi_tab, ki_tab, q_ref, k_ref, v_ref, vprev_ref, kd_ref, ccol_ref, g_ref, o_ref,
              qa_scr, ka_scr, p_scr, m_scr, l_scr, acc_scr, cc_scr, *, t, rq, hp):
    step = pl.program_id(2)
    head0 = pl.program_id(1) * hp
    qi = qi_tab[step]
    ki = ki_tab[step]

    @pl.when(ki == 0)
    def _():
        m_scr[...] = jnp.full_like(m_scr, -jnp.inf)
        l_scr[...] = jnp.zeros_like(l_scr)
        acc_scr[...] = jnp.zeros_like(acc_scr)
        blk = ccol_ref[...]
        lane = lax.broadcasted_iota(jnp.int32, blk.shape, 1)
        for j in range(hp):
            cc_scr[j] = jnp.sum(jnp.where(lane == head0 + j, blk, 0.0), axis=1, keepdims=True)
            qa_scr[j, :, :HEAD_DIM] = q_ref[:, j * HEAD_DIM:(j + 1) * HEAD_DIM]
            qa_scr[j, :, HEAD_DIM:] = (lane < DECAY_TERMS).astype(BF16)

    for j in range(hp):
        ka_scr[j, :, :HEAD_DIM] = k_ref[:, j * HEAD_DIM:(j + 1) * HEAD_DIM]
        ka_scr[j, :, HEAD_DIM:] = kd_ref[j]

    def update(j, r, n_cols, masked, pending):
        rows = pl.ds(r * rq, rq)
        u = lax.dot_general(qa_scr[j, rows, :], ka_scr[j, :n_cols, :], (((1,), (1,)), ((), ())),
                            preferred_element_type=F32)
        if masked:
            tail = u[:, n_cols - rq:]
            row = lax.broadcasted_iota(jnp.int32, tail.shape, 0)
            col = lax.broadcasted_iota(jnp.int32, tail.shape, 1)
            tail = jnp.where(col <= row, tail, -jnp.inf)
            u = tail if n_cols == rq else jnp.concatenate([u[:, :n_cols - rq], tail], axis=1)
        cc = cc_scr[j, rows, :]
        m_prev = m_scr[j, rows, :]
        m_new = jnp.maximum(m_prev, jnp.max(u, axis=1, keepdims=True) + cc)
        p = jnp.exp2(u + (cc - m_new))
        alpha = jnp.exp2(m_prev - m_new)
        l_scr[j, rows, :] = alpha * l_scr[j, rows, :] + jnp.sum(p, axis=1, keepdims=True)
        m_scr[j, rows, :] = m_new
        acc = acc_scr[j, rows, :]
        if pending:
            acc = acc + jnp.dot(p_scr[j, rows, :], vprev_ref[:, j * HEAD_DIM:(j + 1) * HEAD_DIM],
                                preferred_element_type=F32)
        acc_scr[j, rows, :] = alpha * acc
        p_scr[j, rows, :n_cols] = p.astype(BF16)

    n_sub = t // rq

    def tile(masked, pending):
        for r in range(n_sub):
            for j in range(hp):
                update(j, r, (r + 1) * rq if masked else t, masked, pending)

    for diag in (False, True):
        for pending in (False, True):
            first = (qi == 0) if diag else (ki == 0)

            @pl.when(((ki == qi) if diag else (ki < qi)) & (~first if pending else first))
            def _(diag=diag, pending=pending):
                tile(diag, pending)

    @pl.when(ki == qi)
    def _():
        for j in range(hp):
            hs = slice(j * HEAD_DIM, (j + 1) * HEAD_DIM)
            for r in range(n_sub):
                rows = pl.ds(r * rq, rq)
                n_cols = (r + 1) * rq
                acc_scr[j, rows, :] += jnp.dot(p_scr[j, rows, :n_cols], v_ref[:n_cols, hs],
                                               preferred_element_type=F32)
            o = acc_scr[j] / l_scr[j]
            o_ref[:, hs] = (o * jax.nn.sigmoid(g_ref[:, hs].astype(F32))).astype(BF16)


def _fox(q, k, v, kd, c2, g, *, t, rq, hp):
    bsz, seq, width = q.shape
    heads = width // HEAD_DIM
    hw = hp * HEAD_DIM
    pairs = [(qi, ki) for qi in range(seq // t) for ki in range(qi + 1)]
    qi_tab = jnp.asarray(np.array([p[0] for p in pairs], np.int32))
    ki_tab = jnp.asarray(np.array([p[1] for p in pairs], np.int32))
    q_spec = pl.BlockSpec((None, t, hw), lambda b, h, s, qt, kt: (b, qt[s], h))
    kv_spec = pl.BlockSpec((None, t, hw), lambda b, h, s, qt, kt: (b, kt[s], h))
    vprev_spec = pl.BlockSpec((None, t, hw),
                              lambda b, h, s, qt, kt: (b, jnp.maximum(kt[s] - 1, 0), h))
    grid_spec = pltpu.PrefetchScalarGridSpec(
        num_scalar_prefetch=2,
        grid=(bsz, heads // hp, len(pairs)),
        in_specs=[
            q_spec, kv_spec, kv_spec, vprev_spec,
            pl.BlockSpec((None, hp, t, LANES), lambda b, h, s, qt, kt: (b, h, kt[s], 0)),
            pl.BlockSpec((None, t, LANES), lambda b, h, s, qt, kt: (b, qt[s], 0)),
            q_spec,
        ],
        out_specs=q_spec,
        scratch_shapes=[pltpu.VMEM((hp, t, 2 * HEAD_DIM), BF16),
                        pltpu.VMEM((hp, t, 2 * HEAD_DIM), BF16),
                        pltpu.VMEM((hp, t, t), BF16),
                        pltpu.VMEM((hp, t, 1), F32), pltpu.VMEM((hp, t, 1), F32),
                        pltpu.VMEM((hp, t, HEAD_DIM), F32), pltpu.VMEM((hp, t, 1), F32)],
    )
    return pl.pallas_call(
        functools.partial(_fox_body, t=t, rq=rq, hp=hp),
        grid_spec=grid_spec,
        out_shape=jax.ShapeDtypeStruct(q.shape, BF16),
        compiler_params=_params("parallel", "parallel", "arbitrary"),
        name="fox",
    )(qi_tab, ki_tab, q, k, v, v, kd, c2, g)


def _mem_attn_body(q_ref, mk_ref, mv_ref, o_ref):
    for h in range(MEM_HEADS):
        hs = slice(h * HEAD_DIM, (h + 1) * HEAD_DIM)
        s = lax.dot_general(q_ref[:, hs], mk_ref[:, hs], (((1,), (1,)), ((), ())),
                            preferred_element_type=F32)
        e = jnp.exp(s - jnp.max(s, axis=1, keepdims=True))
        p = e / jnp.sum(e, axis=1, keepdims=True)
        o_ref[:, hs] = jnp.dot(p.astype(BF16), mv_ref[:, hs],
                               preferred_element_type=F32).astype(BF16)


def _mem_attn(qm, mk, mv, bsz, seq, *, tm):
    n_mem = mk.shape[0] // bsz
    nt = seq // tm
    return pl.pallas_call(
        _mem_attn_body,
        grid=(bsz, nt),
        in_specs=[pl.BlockSpec((tm, MEM_W), lambda b, i: (b * nt + i, 0)),
                  pl.BlockSpec((n_mem, MEM_W), lambda b, i: (b, 0)),
                  pl.BlockSpec((n_mem, MEM_W), lambda b, i: (b, 0))],
        out_specs=pl.BlockSpec((tm, MEM_W), lambda b, i: (b * nt + i, 0)),
        out_shape=jax.ShapeDtypeStruct(qm.shape, BF16),
        compiler_params=_params("parallel", "arbitrary"),
        name="mem_attn",
    )(qm, mk, mv)


def _out_proj_body(x_ref, o_ref, mo_ref, wo_ref, wm_ref, y_ref):
    y = x_ref[...] + jnp.dot(o_ref[...], wo_ref[...], preferred_element_type=F32)
    y_ref[...] = y + jnp.dot(mo_ref[...], wm_ref[...], preferred_element_type=F32)


def _out_proj(x, o, mo, w_o, w_m, *, tm):
    n, d = x.shape
    return pl.pallas_call(
        _out_proj_body,
        grid=(n // tm,),
        in_specs=[pl.BlockSpec((tm, d), lambda i: (i, 0)),
                  pl.BlockSpec((tm, o.shape[1]), lambda i: (i, 0)),
                  pl.BlockSpec((tm, mo.shape[1]), lambda i: (i, 0)),
                  pl.BlockSpec(w_o.shape, lambda i: (0, 0)),
                  pl.BlockSpec(w_m.shape, lambda i: (0, 0))],
        out_specs=pl.BlockSpec((tm, d), lambda i: (i, 0)),
        out_shape=jax.ShapeDtypeStruct((n, d), F32),
        compiler_params=_params("parallel"),
        name="out_proj",
    )(x, o, mo, w_o, w_m)


def _pad_cols(w, width):
    return jnp.pad(w, ((0, 0), (0, width - w.shape[1])))


def _pad_gla_heads(w, hk):
    lead = w.shape[:-1]
    w = w.reshape(lead + (GLA_HEADS, hk))
    w = jnp.pad(w, [(0, 0)] * len(lead) + [(0, 0), (0, GLA_HK_PAD - hk)])
    return w.reshape(lead + (GLA_HEADS * GLA_HK_PAD,))


def _forward(x, mem, ffn_norm, ffn_w1, ffn_w3, ffn_w2, mix_norm, mem_norm, w_mem_kv,
             mem_q_norm, mem_k_norm, w_out, a_w_in, a_w_gate_up, a_b_gate, a_out_norm,
             b_w_in, b_q_norm, kv_norm, w_kv, b_f, k_norm, *, tiles):
    bsz, seq, d = x.shape
    n = bsz * seq
    depth = ffn_norm.shape[0]
    n_a = a_w_in.shape[0]
    main_w = d - MEM_W
    gla_key = main_w // 2
    hk = gla_key // GLA_HEADS
    fox_heads = main_w // HEAD_DIM
    tn = tiles["proj_tn"]
    attn_scale = HEAD_DIM ** -0.5

    w1 = ffn_w1.astype(BF16)
    w3 = ffn_w3.astype(BF16)
    w2 = ffn_w2.astype(BF16)
    x2 = x.reshape(n, d)
    mem2 = mem.reshape(bsz * mem.shape[1], d)

    def ffn(x2, l, s):
        return _ffn(x2, ffn_norm[l, s], w1, w3, w2, l, s,
                    tm=tiles["ffn_tm"], tf=tiles["ffn_tf"])

    k_sh = v_sh = c2 = k_decay = None
    for l in range(depth):
        if l == n_a:
            w = jnp.concatenate([w_kv[:, :2 * main_w], _pad_cols(w_kv[:, 2 * main_w:], tn)],
                                axis=1).astype(BF16)
            bias = _pad_cols(b_f.reshape(1, fox_heads), LANES)
            k_sh, v_sh, logf = _norm_proj(
                x2, kv_norm, w,
                [(main_w // tn, tn, BF16, _head_norm_ep(0, 1.0)),
                 (main_w // tn, tn, BF16, _plain_ep),
                 (1, LANES, F32, _forget_gate_ep(1))],
                [k_norm.reshape(1, HEAD_DIM), bias],
                tm=tiles["proj_tm"], tn=tn, name="kv_proj")
            c2, k_decay = _seq_cumsum(logf, bsz, seq, fox_heads, ts=tiles["cum_ts"])

        x2 = ffn(x2, l, 0)

        mk, mv = _norm_proj(
            mem2, mem_norm[l], w_mem_kv[l].astype(BF16),
            [(MEM_W // tn, tn, BF16, _head_norm_ep(0, 1.0)),
             (MEM_W // tn, tn, BF16, _plain_ep)],
            [mem_k_norm[l].reshape(1, HEAD_DIM)],
            tm=tiles["mem_tm"], tn=tn, name="mem_kv_proj")
        qm_ep = _head_norm_ep(0, attn_scale)

        if l < n_a:
            wi = a_w_in[l]
            c = np.cumsum([0, gla_key, gla_key, main_w, GLA_GATE_RANK, main_w, MEM_W])
            w = jnp.concatenate([
                _pad_gla_heads(wi[:, c[0]:c[1]], hk),
                _pad_gla_heads(wi[:, c[1]:c[2]], hk),
                wi[:, c[2]:c[3]],
                wi[:, c[4]:c[5]],
                wi[:, c[5]:c[6]],
                _pad_cols(wi[:, c[3]:c[4]], tn),
            ], axis=1).astype(BF16)
            kw = GLA_HEADS * GLA_HK_PAD
            w_up = jnp.pad(_pad_gla_heads(a_w_gate_up[l], hk),
                           ((0, tn - GLA_GATE_RANK), (0, 0))).astype(BF16)
            b_gate = _pad_gla_heads(a_b_gate[l].reshape(1, gla_key), hk)
            q, k, v, g, qm, la = _norm_proj(
                x2, mix_norm[l], w,
                [(kw // tn, tn, BF16, _scale_ep(hk ** -0.5)),
                 (kw // tn, tn, F32, _plain_ep),
                 (main_w // tn, tn, BF16, _plain_ep),
                 (main_w // tn, tn, BF16, _plain_ep),
                 (MEM_W // tn, tn, BF16, qm_ep),
                 (1, kw, F32, _gla_gate_ep(1, 2))],
                [mem_q_norm[l].reshape(1, HEAD_DIM), w_up, b_gate],
                tm=tiles["proj_tm"], tn=tn, name="gla_in_proj")
            o = _gla(q, k, la, v, g, a_out_norm[l], bsz, seq, n_chunk=tiles["gla_chunks"])
        else:
            jl = l - n_a
            q, g, qm = _norm_proj(
                x2, mix_norm[l], b_w_in[jl].astype(BF16),
                [(main_w // tn, tn, BF16, _head_norm_ep(0, attn_scale * LOG2E)),
                 (main_w // tn, tn, BF16, _plain_ep),
                 (MEM_W // tn, tn, BF16, _head_norm_ep(1, attn_scale))],
                [b_q_norm[jl].reshape(1, HEAD_DIM), mem_q_norm[l].reshape(1, HEAD_DIM)],
                tm=tiles["proj_tm"], tn=tn, name="fox_in_proj")
            o = _fox(q.reshape(bsz, seq, main_w), k_sh.reshape(bsz, seq, main_w),
                     v_sh.reshape(bsz, seq, main_w), k_decay, c2,
                     g.reshape(bsz, seq, main_w), t=tiles["fox_t"], rq=tiles["fox_rq"], hp=tiles["fox_hp"])
            o = o.reshape(n, main_w)

        mo = _mem_attn(qm, mk, mv, bsz, seq, tm=tiles["attn_tm"])
        wo = w_out[l].astype(BF16)
        x2 = _out_proj(x2, o, mo, wo[:main_w], wo[main_w:], tm=tiles["out_tm"])
        x2 = ffn(x2, l, 1)
    return x2.reshape(bsz, seq, d)


_TILES = dict(ffn_tm=512, ffn_tf=512, proj_tm=1024, proj_tn=512, mem_tm=512, cum_ts=512,
              gla_chunks=8, fox_t=1024, fox_rq=256, fox_hp=2, attn_tm=1024, out_tm=512)


def kernel(x, mem, ffn_norm, ffn_w1, ffn_w3, ffn_w2, mix_norm, mem_norm, w_mem_kv, mem_q_norm, mem_k_norm, w_out, a_w_in, a_w_gate_up, a_b_gate, a_out_norm, b_w_in, b_q_norm, kv_norm, w_kv, b_f, k_norm):
    return _forward(x, mem, ffn_norm, ffn_w1, ffn_w3, ffn_w2, mix_norm, mem_norm, w_mem_kv,
                    mem_q_norm, mem_k_norm, w_out, a_w_in, a_w_gate_up, a_b_gate, a_out_norm,
                    b_w_in, b_q_norm, kv_norm, w_kv, b_f, k_norm, tiles=_TILES)
```

```python
import functools

import numpy as np
import jax
import jax.numpy as jnp
from jax import lax
from jax.experimental import pallas as pl
from jax.experimental.pallas import tpu as pltpu

F32 = jnp.float32
BF16 = jnp.bfloat16

CHUNK = 64
HEAD_DIM = 128
MEM_HEADS = 4
MEM_W = MEM_HEADS * HEAD_DIM
GLA_HEADS = 4
GLA_GATE_RANK = 16
GLA_GATE_NORMALIZER = 16.0
EPS = 1e-6
LOG2E = 1.4426950408889634
DECAY_TERMS = 3

LANES = 128
V7X_VMEM_LIMIT_BYTES = 56 * 1024 * 1024

GLA_HK_PAD = 256


def _rms(x, gain):
    return x * lax.rsqrt(jnp.mean(x * x, axis=-1, keepdims=True) + EPS) * gain


def _log_sigmoid(x):
    return jnp.minimum(x, 0.0) - jnp.log1p(jnp.exp(-jnp.abs(x)))


def _silu(x):
    return x * jax.nn.sigmoid(x)


def _split3(x):
    hi = x.astype(BF16)
    r1 = x - hi.astype(F32)
    mid = r1.astype(BF16)
    lo = (r1 - mid.astype(F32)).astype(BF16)
    return hi, mid, lo


def _cumsum_rows(x):
    n = x.shape[0]
    row = lax.broadcasted_iota(jnp.int32, (n, n), 0)
    col = lax.broadcasted_iota(jnp.int32, (n, n), 1)
    tri = (row >= col).astype(BF16)
    return sum(jnp.dot(tri, term, preferred_element_type=F32) for term in _split3(x))


def _params(*semantics):
    return pltpu.CompilerParams(dimension_semantics=semantics,
                                vmem_limit_bytes=V7X_VMEM_LIMIT_BYTES)


def _ffn_body(x_ref, g_ref, w1_ref, w3_ref, w2_ref, o_ref, h_scr):
    def ff_tile(h):
        a = jnp.dot(h, w1_ref[...], preferred_element_type=F32)
        b = jnp.dot(h, w3_ref[...], preferred_element_type=F32)
        act = (_silu(a) * b * 0.5).astype(BF16)
        return jnp.dot(act, w2_ref[...], preferred_element_type=F32)

    @pl.when(pl.program_id(1) == 0)
    def _():
        x = x_ref[...]
        h = _rms(x, g_ref[...]).astype(BF16)
        h_scr[...] = h
        o_ref[...] = x + ff_tile(h)

    @pl.when(pl.program_id(1) > 0)
    def _():
        o_ref[...] += ff_tile(h_scr[...])


def _col_tiles(w, tn):
    lead, (d, c) = w.shape[:-2], w.shape[-2:]
    w = w.reshape(lead + (d, c // tn, tn))
    return jnp.swapaxes(w, -3, -2)


def _ffn(x, gain, w1, w3, w2, layer, half, *, tm):
    n, d = x.shape
    nf, tf = w1.shape[2], w1.shape[4]
    return pl.pallas_call(
        _ffn_body,
        grid=(n // tm, nf),
        in_specs=[
            pl.BlockSpec((tm, d), lambda i, j: (i, 0)),
            pl.BlockSpec((1, d), lambda i, j: (0, 0)),
            pl.BlockSpec((None, None, None, d, tf), lambda i, j: (layer, half, j, 0, 0)),
            pl.BlockSpec((None, None, None, d, tf), lambda i, j: (layer, half, j, 0, 0)),
            pl.BlockSpec((None, None, tf, d), lambda i, j: (layer, half, j, 0)),
        ],
        out_specs=pl.BlockSpec((tm, d), lambda i, j: (i, 0)),
        out_shape=jax.ShapeDtypeStruct((n, d), F32),
        scratch_shapes=[pltpu.VMEM((tm, d), BF16)],
        compiler_params=_params("parallel", "arbitrary"),
        name="ffn",
    )(x, gain.reshape(1, d), w1, w3, w2)


def _norm_proj_body(*refs, groups, n_extra):
    x_ref, g_ref, w_ref = refs[:3]
    extra = refs[3:3 + n_extra]
    outs = refs[3 + n_extra:3 + n_extra + len(groups)]
    h_scr = refs[-1]
    j = pl.program_id(1)

    def tile(h, fn, o_ref):
        acc = jnp.dot(h, w_ref[...], preferred_element_type=F32)
        o_ref[...] = fn(acc, extra).astype(o_ref.dtype)

    @pl.when(j == 0)
    def _():
        h = _rms(x_ref[...], g_ref[...]).astype(BF16)
        h_scr[...] = h
        tile(h, groups[0][3], outs[0])

    start = 0
    for (n_tiles, _, _, fn), o_ref in zip(groups, outs):
        @pl.when((j >= max(start, 1)) & (j < start + n_tiles))
        def _(fn=fn, o_ref=o_ref):
            tile(h_scr[...], fn, o_ref)
        start += n_tiles


def _norm_proj(x, gain, w, groups, extras, *, tm, tn, name):
    n, d = x.shape
    assert w.shape[1] == tn * sum(g[0] for g in groups)
    in_specs = [
        pl.BlockSpec((tm, d), lambda i, j: (i, 0)),
        pl.BlockSpec((1, d), lambda i, j: (0, 0)),
        pl.BlockSpec((None, d, tn), lambda i, j: (j, 0, 0)),
    ] + [pl.BlockSpec(e.shape, lambda i, j: (0, 0)) for e in extras]
    out_specs, out_shapes = [], []
    start = 0
    for n_tiles, width, dtype, _ in groups:
        out_specs.append(pl.BlockSpec(
            (tm, width),
            lambda i, j, s=start, c=n_tiles: (i, jnp.clip(j - s, 0, c - 1))))
        out_shapes.append(jax.ShapeDtypeStruct((n, width * n_tiles), dtype))
        start += n_tiles
    return pl.pallas_call(
        functools.partial(_norm_proj_body, groups=groups, n_extra=len(extras)),
        grid=(n // tm, start),
        in_specs=in_specs,
        out_specs=out_specs,
        out_shape=out_shapes,
        scratch_shapes=[pltpu.VMEM((tm, d), BF16)],
        compiler_params=_params("parallel", "arbitrary"),
        name=name,
    )(x, gain.reshape(1, d), _col_tiles(w, tn), *extras)


def _scale_ep(scale):
    return lambda acc, extra: acc * scale


def _plain_ep(acc, extra):
    return acc


def _head_norm_ep(gain_idx, scale):
    def fn(acc, extra):
        gain = extra[gain_idx][...]
        parts = []
        for h in range(acc.shape[1] // HEAD_DIM):
            a = acc[:, h * HEAD_DIM:(h + 1) * HEAD_DIM]
            parts.append(_rms(a, gain) * scale)
        return jnp.concatenate(parts, axis=-1)
    return fn


def _gla_gate_ep(w_up_idx, bias_idx):
    def fn(acc, extra):
        z = jnp.dot(acc.astype(BF16), extra[w_up_idx][...], preferred_element_type=F32)
        return _log_sigmoid(z + extra[bias_idx][...]) / GLA_GATE_NORMALIZER
    return fn


def _forget_gate_ep(bias_idx):
    def fn(acc, extra):
        return _log_sigmoid(acc[:, :LANES] + extra[bias_idx][...])
    return fn


def _cumsum_body(lf_ref, col_ref, kd_ref, carry):
    @pl.when(pl.program_id(1) == 0)
    def _():
        carry[...] = jnp.zeros_like(carry)

    cum = _cumsum_rows(lf_ref[...]) + carry[...]
    carry[...] = cum[cum.shape[0] - 1:, :]
    c2 = cum * LOG2E
    col_ref[...] = c2
    terms = [term.astype(F32) for term in _split3(-c2)]
    lane = lax.broadcasted_iota(jnp.int32, c2.shape, 1)
    for h in range(kd_ref.shape[0]):
        tile = jnp.zeros_like(c2)
        for i, term in enumerate(terms):
            tile = jnp.where(lane == i, term[:, h:h + 1], tile)
        kd_ref[h] = tile.astype(BF16)


def _seq_cumsum(logf, bsz, seq, heads, *, ts):
    return pl.pallas_call(
        _cumsum_body,
        grid=(bsz, seq // ts),
        in_specs=[pl.BlockSpec((None, ts, LANES), lambda b, s: (b, s, 0))],
        out_specs=[pl.BlockSpec((None, ts, LANES), lambda b, s: (b, s, 0)),
                   pl.BlockSpec((None, heads, ts, LANES), lambda b, s: (b, 0, s, 0))],
        out_shape=[jax.ShapeDtypeStruct((bsz, seq, LANES), F32),
                   jax.ShapeDtypeStruct((bsz, heads, seq, LANES), BF16)],
        scratch_shapes=[pltpu.VMEM((1, LANES), F32)],
        compiler_params=_params("parallel", "arbitrary"),
        name="forget_cumsum",
    )(logf.reshape(bsz, seq, LANES))


def _gla_body(q_ref, k_ref, la_ref, v_ref, g_ref, gain_ref, o_ref, st_ref, *, n_chunk, hv):
    @pl.when(pl.program_id(1) == 0)
    def _():
        st_ref[...] = jnp.zeros_like(st_ref)

    gain = gain_ref[...]
    for c in range(n_chunk):
        rows = pl.ds(c * CHUNK, CHUNK)
        cum = _cumsum_rows(la_ref[rows, :])
        total = cum[CHUNK - 1:, :]
        k_dec = (k_ref[rows, :] * jnp.exp(total - cum)).astype(BF16)
        a_chunk = jnp.exp(total)
        for h in range(GLA_HEADS):
            ks = slice(h * GLA_HK_PAD, (h + 1) * GLA_HK_PAD)
            vs = slice(h * hv, (h + 1) * hv)
            upd = lax.dot_general(v_ref[rows, vs], k_dec[:, ks],
                                  (((0,), (0,)), ((), ())), preferred_element_type=F32)
            st = a_chunk[:, ks] * st_ref[h] + upd
            st_ref[h] = st
            o = lax.dot_general(q_ref[rows, ks], st.astype(BF16),
                                (((1,), (1,)), ((), ())), preferred_element_type=F32)
            o = _rms(o, gain) * _silu(g_ref[rows, vs].astype(F32))
            o_ref[rows, vs] = o.astype(BF16)


def _gla(q, k, la, v, g, gain, bsz, seq, *, n_chunk):
    hv = v.shape[1] // GLA_HEADS
    kw = q.shape[1]
    tc = n_chunk * CHUNK
    nt = seq // tc

    def spec(width):
        return pl.BlockSpec((tc, width), lambda b, c: (b * nt + c, 0))

    return pl.pallas_call(
        functools.partial(_gla_body, n_chunk=n_chunk, hv=hv),
        grid=(bsz, nt),
        in_specs=[spec(kw), spec(kw), spec(kw), spec(v.shape[1]), spec(v.shape[1]),
                  pl.BlockSpec((1, hv), lambda b, c: (0, 0))],
        out_specs=spec(v.shape[1]),
        out_shape=jax.ShapeDtypeStruct(v.shape, BF16),
        scratch_shapes=[pltpu.VMEM((GLA_HEADS, hv, GLA_HK_PAD), F32)],
        compiler_params=_params("parallel", "arbitrary"),
        name="gla",
    )(q, k, la, v, g, gain.reshape(1, hv))


def _fox_body(qi_tab, ki_tab, q_ref, k_ref, v_ref, vprev_ref, kd_ref, ccol_ref, g_ref, o_ref,
              qa_scr, ka_scr, p_scr, m_scr, l_scr, acc_scr, cc_scr, *, t, rq, hp):
    step = pl.program_id(2)
    head0 = pl.program_id(1) * hp
    qi = qi_tab[step]
    ki = ki_tab[step]

    @pl.when(ki == 0)
    def _():
        m_scr[...] = jnp.full_like(m_scr, -jnp.inf)
        l_scr[...] = jnp.zeros_like(l_scr)
        acc_scr[...] = jnp.zeros_like(acc_scr)
        blk = ccol_ref[...]
        lane = lax.broadcasted_iota(jnp.int32, blk.shape, 1)
        for j in range(hp):
            cc_scr[j] = jnp.sum(jnp.where(lane == head0 + j, blk, 0.0), axis=1, keepdims=True)
            qa_scr[j, :, :HEAD_DIM] = q_ref[:, j * HEAD_DIM:(j + 1) * HEAD_DIM]
            qa_scr[j, :, HEAD_DIM:] = (lane < DECAY_TERMS).astype(BF16)

    for j in range(hp):
        ka_scr[j, :, :HEAD_DIM] = k_ref[:, j * HEAD_DIM:(j + 1) * HEAD_DIM]
        ka_scr[j, :, HEAD_DIM:] = kd_ref[j]

    def update(j, r, n_cols, masked, pending):
        rows = pl.ds(r * rq, rq)
        u = lax.dot_general(qa_scr[j, rows, :], ka_scr[j, :n_cols, :], (((1,), (1,)), ((), ())),
                            preferred_element_type=F32)
        if masked:
            tail = u[:, n_cols - rq:]
            row = lax.broadcasted_iota(jnp.int32, tail.shape, 0)
            col = lax.broadcasted_iota(jnp.int32, tail.shape, 1)
            tail = jnp.where(col <= row, tail, -jnp.inf)
            u = tail if n_cols == rq else jnp.concatenate([u[:, :n_cols - rq], tail], axis=1)
        cc = cc_scr[j, rows, :]
        m_prev = m_scr[j, rows, :]
        m_new = jnp.maximum(m_prev, jnp.max(u, axis=1, keepdims=True) + cc)
        p = jnp.exp2(u + (cc - m_new))
        alpha = jnp.exp2(m_prev - m_new)
        l_scr[j, rows, :] = alpha * l_scr[j, rows, :] + jnp.sum(p, axis=1, keepdims=True)
        m_scr[j, rows, :] = m_new
        acc = acc_scr[j, rows, :]
        if pending:
            acc = acc + jnp.dot(p_scr[j, rows, :], vprev_ref[:, j * HEAD_DIM:(j + 1) * HEAD_DIM],
                                preferred_element_type=F32)
        acc_scr[j, rows, :] = alpha * acc
        p_scr[j, rows, :n_cols] = p.astype(BF16)

    n_sub = t // rq

    def tile(masked, pending):
        for r in range(n_sub):
            for j in range(hp):
                update(j, r, (r + 1) * rq if masked else t, masked, pending)

    for diag in (False, True):
        for pending in (False, True):
            first = (qi == 0) if diag else (ki == 0)

            @pl.when(((ki == qi) if diag else (ki < qi)) & (~first if pending else first))
            def _(diag=diag, pending=pending):
                tile(diag, pending)

    @pl.when(ki == qi)
    def _():
        for j in range(hp):
            hs = slice(j * HEAD_DIM, (j + 1) * HEAD_DIM)
            for r in range(n_sub):
                rows = pl.ds(r * rq, rq)
                n_cols = (r + 1) * rq
                acc_scr[j, rows, :] += jnp.dot(p_scr[j, rows, :n_cols], v_ref[:n_cols, hs],
                                               preferred_element_type=F32)
            o = acc_scr[j] / l_scr[j]
            o_ref[:, hs] = (o * jax.nn.sigmoid(g_ref[:, hs].astype(F32))).astype(BF16)


def _fox(q, k, v, kd, c2, g, *, t, rq, hp):
    bsz, seq, width = q.shape
    heads = width // HEAD_DIM
    hw = hp * HEAD_DIM
    pairs = [(qi, ki) for qi in range(seq // t) for ki in range(qi + 1)]
    qi_tab = jnp.asarray(np.array([p[0] for p in pairs], np.int32))
    ki_tab = jnp.asarray(np.array([p[1] for p in pairs], np.int32))
    q_spec = pl.BlockSpec((None, t, hw), lambda b, h, s, qt, kt: (b, qt[s], h))
    kv_spec = pl.BlockSpec((None, t, hw), lambda b, h, s, qt, kt: (b, kt[s], h))
    vprev_spec = pl.BlockSpec((None, t, hw),
                              lambda b, h, s, qt, kt: (b, jnp.maximum(kt[s] - 1, 0), h))
    grid_spec = pltpu.PrefetchScalarGridSpec(
        num_scalar_prefetch=2,
        grid=(bsz, heads // hp, len(pairs)),
        in_specs=[
            q_spec, kv_spec, kv_spec, vprev_spec,
            pl.BlockSpec((None, hp, t, LANES), lambda b, h, s, qt, kt: (b, h, kt[s], 0)),
            pl.BlockSpec((None, t, LANES), lambda b, h, s, qt, kt: (b, qt[s], 0)),
            q_spec,
        ],
        out_specs=q_spec,
        scratch_shapes=[pltpu.VMEM((hp, t, 2 * HEAD_DIM), BF16),
                        pltpu.VMEM((hp, t, 2 * HEAD_DIM), BF16),
                        pltpu.VMEM((hp, t, t), BF16),
                        pltpu.VMEM((hp, t, 1), F32), pltpu.VMEM((hp, t, 1), F32),
                        pltpu.VMEM((hp, t, HEAD_DIM), F32), pltpu.VMEM((hp, t, 1), F32)],
    )
    return pl.pallas_call(
        functools.partial(_fox_body, t=t, rq=rq, hp=hp),
        grid_spec=grid_spec,
        out_shape=jax.ShapeDtypeStruct(q.shape, BF16),
        compiler_params=_params("parallel", "parallel", "arbitrary"),
        name="fox",
    )(qi_tab, ki_tab, q, k, v, v, kd, c2, g)


def _mem_attn_body(q_ref, mk_ref, mv_ref, o_ref):
    for h in range(MEM_HEADS):
        hs = slice(h * HEAD_DIM, (h + 1) * HEAD_DIM)
        s = lax.dot_general(q_ref[:, hs], mk_ref[:, hs], (((1,), (1,)), ((), ())),
                            preferred_element_type=F32)
        e = jnp.exp(s - jnp.max(s, axis=1, keepdims=True))
        p = e / jnp.sum(e, axis=1, keepdims=True)
        o_ref[:, hs] = jnp.dot(p.astype(BF16), mv_ref[:, hs],
                               preferred_element_type=F32).astype(BF16)


def _mem_attn(qm, mk, mv, bsz, seq, *, tm):
    n_mem = mk.shape[0] // bsz
    nt = seq // tm
    return pl.pallas_call(
        _mem_attn_body,
        grid=(bsz, nt),
        in_specs=[pl.BlockSpec((tm, MEM_W), lambda b, i: (b * nt + i, 0)),
                  pl.BlockSpec((n_mem, MEM_W), lambda b, i: (b, 0)),
                  pl.BlockSpec((n_mem, MEM_W), lambda b, i: (b, 0))],
        out_specs=pl.BlockSpec((tm, MEM_W), lambda b, i: (b * nt + i, 0)),
        out_shape=jax.ShapeDtypeStruct(qm.shape, BF16),
        compiler_params=_params("parallel", "arbitrary"),
        name="mem_attn",
    )(qm, mk, mv)


def _out_proj_body(x_ref, o_ref, mo_ref, wo_ref, wm_ref, y_ref):
    y = x_ref[...] + jnp.dot(o_ref[...], wo_ref[...], preferred_element_type=F32)
    y_ref[...] = y + jnp.dot(mo_ref[...], wm_ref[...], preferred_element_type=F32)


def _out_proj(x, o, mo, w_o, w_m, *, tm):
    n, d = x.shape
    return pl.pallas_call(
        _out_proj_body,
        grid=(n // tm,),
        in_specs=[pl.BlockSpec((tm, d), lambda i: (i, 0)),
                  pl.BlockSpec((tm, o.shape[1]), lambda i: (i, 0)),
                  pl.BlockSpec((tm, mo.shape[1]), lambda i: (i, 0)),
                  pl.BlockSpec(w_o.shape, lambda i: (0, 0)),
                  pl.BlockSpec(w_m.shape, lambda i: (0, 0))],
        out_specs=pl.BlockSpec((tm, d), lambda i: (i, 0)),
        out_shape=jax.ShapeDtypeStruct((n, d), F32),
        compiler_params=_params("parallel"),
        name="out_proj",
    )(x, o, mo, w_o, w_m)


def _pad_cols(w, width):
    return jnp.pad(w, ((0, 0), (0, width - w.shape[1])))


def _pad_gla_heads(w, hk):
    lead = w.shape[:-1]
    w = w.reshape(lead + (GLA_HEADS, hk))
    w = jnp.pad(w, [(0, 0)] * len(lead) + [(0, 0), (0, GLA_HK_PAD - hk)])
    return w.reshape(lead + (GLA_HEADS * GLA_HK_PAD,))


def _forward(x, mem, ffn_norm, ffn_w1, ffn_w3, ffn_w2, mix_norm, mem_norm, w_mem_kv,
             mem_q_norm, mem_k_norm, w_out, a_w_in, a_w_gate_up, a_b_gate, a_out_norm,
             b_w_in, b_q_norm, kv_norm, w_kv, b_f, k_norm, *, tiles):
    bsz, seq, d = x.shape
    n = bsz * seq
    depth = ffn_norm.shape[0]
    n_a = a_w_in.shape[0]
    main_w = d - MEM_W
    gla_key = main_w // 2
    hk = gla_key // GLA_HEADS
    fox_heads = main_w // HEAD_DIM
    tn = tiles["proj_tn"]
    attn_scale = HEAD_DIM ** -0.5

    w1 = _col_tiles(ffn_w1.astype(BF16), tiles["ffn_tf"])
    w3 = _col_tiles(ffn_w3.astype(BF16), tiles["ffn_tf"])
    w2 = ffn_w2.astype(BF16)
    x2 = x.reshape(n, d)
    mem2 = mem.reshape(bsz * mem.shape[1], d)

    def ffn(x2, l, s):
        return _ffn(x2, ffn_norm[l, s], w1, w3, w2, l, s, tm=tiles["ffn_tm"])

    k_sh = v_sh = c2 = k_decay = None
    for l in range(depth):
        if l == n_a:
            w = jnp.concatenate([w_kv[:, :2 * main_w], _pad_cols(w_kv[:, 2 * main_w:], tn)],
                                axis=1).astype(BF16)
            bias = _pad_cols(b_f.reshape(1, fox_heads), LANES)
            k_sh, v_sh, logf = _norm_proj(
                x2, kv_norm, w,
                [(main_w // tn, tn, BF16, _head_norm_ep(0, 1.0)),
                 (main_w // tn, tn, BF16, _plain_ep),
                 (1, LANES, F32, _forget_gate_ep(1))],
                [k_norm.reshape(1, HEAD_DIM), bias],
                tm=tiles["proj_tm"], tn=tn, name="kv_proj")
            c2, k_decay = _seq_cumsum(logf, bsz, seq, fox_heads, ts=tiles["cum_ts"])

        x2 = ffn(x2, l, 0)

        mk, mv = _norm_proj(
            mem2, mem_norm[l], w_mem_kv[l].astype(BF16),
            [(MEM_W // tn, tn, BF16, _head_norm_ep(0, 1.0)),
             (MEM_W // tn, tn, BF16, _plain_ep)],
            [mem_k_norm[l].reshape(1, HEAD_DIM)],
            tm=tiles["mem_tm"], tn=tn, name="mem_kv_proj")
        qm_ep = _head_norm_ep(0, attn_scale)

        if l < n_a:
            wi = a_w_in[l]
            c = np.cumsum([0, gla_key, gla_key, main_w, GLA_GATE_RANK, main_w, MEM_W])
            w = jnp.concatenate([
                _pad_gla_heads(wi[:, c[0]:c[1]], hk),
                _pad_gla_heads(wi[:, c[1]:c[2]], hk),
                wi[:, c[2]:c[3]],
                wi[:, c[4]:c[5]],
                wi[:, c[5]:c[6]],
                _pad_cols(wi[:, c[3]:c[4]], tn),
            ], axis=1).astype(BF16)
            kw = GLA_HEADS * GLA_HK_PAD
            w_up = jnp.pad(_pad_gla_heads(a_w_gate_up[l], hk),
                           ((0, tn - GLA_GATE_RANK), (0, 0))).astype(BF16)
            b_gate = _pad_gla_heads(a_b_gate[l].reshape(1, gla_key), hk)
            q, k, v, g, qm, la = _norm_proj(
                x2, mix_norm[l], w,
                [(kw // tn, tn, BF16, _scale_ep(hk ** -0.5)),
                 (kw // tn, tn, F32, _plain_ep),
                 (main_w // tn, tn, BF16, _plain_ep),
                 (main_w // tn, tn, BF16, _plain_ep),
                 (MEM_W // tn, tn, BF16, qm_ep),
                 (1, kw, F32, _gla_gate_ep(1, 2))],
                [mem_q_norm[l].reshape(1, HEAD_DIM), w_up, b_gate],
                tm=tiles["proj_tm"], tn=tn, name="gla_in_proj")
            o = _gla(q, k, la, v, g, a_out_norm[l], bsz, seq, n_chunk=tiles["gla_chunks"])
        else:
            jl = l - n_a
            q, g, qm = _norm_proj(
                x2, mix_norm[l], b_w_in[jl].astype(BF16),
                [(main_w // tn, tn, BF16, _head_norm_ep(0, attn_scale * LOG2E)),
                 (main_w // tn, tn, BF16, _plain_ep),
                 (MEM_W // tn, tn, BF16, _head_norm_ep(1, attn_scale))],
                [b_q_norm[jl].reshape(1, HEAD_DIM), mem_q_norm[l].reshape(1, HEAD_DIM)],
                tm=tiles["proj_tm"], tn=tn, name="fox_in_proj")
            o = _fox(q.reshape(bsz, seq, main_w), k_sh.reshape(bsz, seq, main_w),
                     v_sh.reshape(bsz, seq, main_w), k_decay, c2,
                     g.reshape(bsz, seq, main_w), t=tiles["fox_t"], rq=tiles["fox_rq"], hp=tiles["fox_hp"])
            o = o.reshape(n, main_w)

        mo = _mem_attn(qm, mk, mv, bsz, seq, tm=tiles["attn_tm"])
        wo = w_out[l].astype(BF16)
        x2 = _out_proj(x2, o, mo, wo[:main_w], wo[main_w:], tm=tiles["out_tm"])
        x2 = ffn(x2, l, 1)
    return x2.reshape(bsz, seq, d)


_TILES = dict(ffn_tm=512, ffn_tf=512, proj_tm=1024, proj_tn=512, mem_tm=512, cum_ts=512,
              gla_chunks=8, fox_t=1024, fox_rq=256, fox_hp=2, attn_tm=1024, out_tm=512)


def kernel(x, mem, ffn_norm, ffn_w1, ffn_w3, ffn_w2, mix_norm, mem_norm, w_mem_kv, mem_q_norm, mem_k_norm, w_out, a_w_in, a_w_gate_up, a_b_gate, a_out_norm, b_w_in, b_q_norm, kv_norm, w_kv, b_f, k_norm):
    return _forward(x, mem, ffn_norm, ffn_w1, ffn_w3, ffn_w2, mix_norm, mem_norm, w_mem_kv,
                    mem_q_norm, mem_k_norm, w_out, a_w_in, a_w_gate_up, a_b_gate, a_out_norm,
                    b_w_in, b_q_norm, kv_norm, w_kv, b_f, k_norm, tiles=_TILES)
```

```python
import functools

import numpy as np
import jax
import jax.numpy as jnp
from jax import lax
from jax.experimental import pallas as pl
from jax.experimental.pallas import tpu as pltpu

F32 = jnp.float32
BF16 = jnp.bfloat16

CHUNK = 64
HEAD_DIM = 128
MEM_HEADS = 4
MEM_W = MEM_HEADS * HEAD_DIM
GLA_HEADS = 4
GLA_GATE_RANK = 16
GLA_GATE_NORMALIZER = 16.0
EPS = 1e-6
LOG2E = 1.4426950408889634
DECAY_TERMS = 3

LANES = 128
V7X_VMEM_LIMIT_BYTES = 56 * 1024 * 1024

GLA_HK_PAD = 256


def _rms(x, gain):
    return x * lax.rsqrt(jnp.mean(x * x, axis=-1, keepdims=True) + EPS) * gain


def _log_sigmoid(x):
    return jnp.minimum(x, 0.0) - jnp.log1p(jnp.exp(-jnp.abs(x)))


def _silu(x):
    return x * jax.nn.sigmoid(x)


def _split3(x):
    hi = x.astype(BF16)
    r1 = x - hi.astype(F32)
    mid = r1.astype(BF16)
    lo = (r1 - mid.astype(F32)).astype(BF16)
    return hi, mid, lo


def _cumsum_rows(x):
    n = x.shape[0]
    row = lax.broadcasted_iota(jnp.int32, (n, n), 0)
    col = lax.broadcasted_iota(jnp.int32, (n, n), 1)
    tri = (row >= col).astype(BF16)
    return sum(jnp.dot(tri, term, preferred_element_type=F32) for term in _split3(x))


def _params(*semantics):
    return pltpu.CompilerParams(dimension_semantics=semantics,
                                vmem_limit_bytes=V7X_VMEM_LIMIT_BYTES)


def _ffn_body(x_ref, g_ref, w1_ref, w3_ref, w2_ref, o_ref, h_scr):
    def ff_tile(h):
        a = jnp.dot(h, w1_ref[...], preferred_element_type=F32)
        b = jnp.dot(h, w3_ref[...], preferred_element_type=F32)
        act = (_silu(a) * b * 0.5).astype(BF16)
        return jnp.dot(act, w2_ref[...], preferred_element_type=F32)

    @pl.when(pl.program_id(1) == 0)
    def _():
        x = x_ref[...]
        h = _rms(x, g_ref[...]).astype(BF16)
        h_scr[...] = h
        o_ref[...] = x + ff_tile(h)

    @pl.when(pl.program_id(1) > 0)
    def _():
        o_ref[...] += ff_tile(h_scr[...])


def _ffn(x, gain, w1, w3, w2, layer, half, *, tm, tf):
    n, d = x.shape
    f = w1.shape[-1]
    return pl.pallas_call(
        _ffn_body,
        grid=(n // tm, f // tf),
        in_specs=[
            pl.BlockSpec((tm, d), lambda i, j: (i, 0)),
            pl.BlockSpec((1, d), lambda i, j: (0, 0)),
            pl.BlockSpec((None, None, d, tf), lambda i, j: (layer, half, 0, j)),
            pl.BlockSpec((None, None, d, tf), lambda i, j: (layer, half, 0, j)),
            pl.BlockSpec((None, None, tf, d), lambda i, j: (layer, half, j, 0)),
        ],
        out_specs=pl.BlockSpec((tm, d), lambda i, j: (i, 0)),
        out_shape=jax.ShapeDtypeStruct((n, d), F32),
        scratch_shapes=[pltpu.VMEM((tm, d), BF16)],
        compiler_params=_params("parallel", "arbitrary"),
        name="ffn",
    )(x, gain.reshape(1, d), w1, w3, w2)


def _norm_proj_body(*refs, groups, n_extra):
    x_ref, g_ref, w_ref = refs[:3]
    extra = refs[3:3 + n_extra]
    outs = refs[3 + n_extra:3 + n_extra + len(groups)]
    h_scr = refs[-1]
    j = pl.program_id(1)

    def tile(h, fn, o_ref):
        acc = jnp.dot(h, w_ref[...], preferred_element_type=F32)
        o_ref[...] = fn(acc, extra).astype(o_ref.dtype)

    @pl.when(j == 0)
    def _():
        h = _rms(x_ref[...], g_ref[...]).astype(BF16)
        h_scr[...] = h
        tile(h, groups[0][3], outs[0])

    start = 0
    for (n_tiles, _, _, fn), o_ref in zip(groups, outs):
        @pl.when((j >= max(start, 1)) & (j < start + n_tiles))
        def _(fn=fn, o_ref=o_ref):
            tile(h_scr[...], fn, o_ref)
        start += n_tiles


def _norm_proj(x, gain, w, groups, extras, *, tm, tn, name):
    n, d = x.shape
    assert w.shape[1] == tn * sum(g[0] for g in groups)
    in_specs = [
        pl.BlockSpec((tm, d), lambda i, j: (i, 0)),
        pl.BlockSpec((1, d), lambda i, j: (0, 0)),
        pl.BlockSpec((d, tn), lambda i, j: (0, j)),
    ] + [pl.BlockSpec(e.shape, lambda i, j: (0, 0)) for e in extras]
    out_specs, out_shapes = [], []
    start = 0
    for n_tiles, width, dtype, _ in groups:
        out_specs.append(pl.BlockSpec(
            (tm, width),
            lambda i, j, s=start, c=n_tiles: (i, jnp.clip(j - s, 0, c - 1))))
        out_shapes.append(jax.ShapeDtypeStruct((n, width * n_tiles), dtype))
        start += n_tiles
    return pl.pallas_call(
        functools.partial(_norm_proj_body, groups=groups, n_extra=len(extras)),
        grid=(n // tm, start),
        in_specs=in_specs,
        out_specs=out_specs,
        out_shape=out_shapes,
        scratch_shapes=[pltpu.VMEM((tm, d), BF16)],
        compiler_params=_params("parallel", "arbitrary"),
        name=name,
    )(x, gain.reshape(1, d), w, *extras)


def _scale_ep(scale):
    return lambda acc, extra: acc * scale


def _plain_ep(acc, extra):
    return acc


def _head_norm_ep(gain_idx, scale):
    def fn(acc, extra):
        gain = extra[gain_idx][...]
        parts = []
        for h in range(acc.shape[1] // HEAD_DIM):
            a = acc[:, h * HEAD_DIM:(h + 1) * HEAD_DIM]
            parts.append(_rms(a, gain) * scale)
        return jnp.concatenate(parts, axis=-1)
    return fn


def _gla_gate_ep(w_up_idx, bias_idx):
    def fn(acc, extra):
        z = jnp.dot(acc.astype(BF16), extra[w_up_idx][...], preferred_element_type=F32)
        return _log_sigmoid(z + extra[bias_idx][...]) / GLA_GATE_NORMALIZER
    return fn


def _forget_gate_ep(bias_idx):
    def fn(acc, extra):
        return _log_sigmoid(acc[:, :LANES] + extra[bias_idx][...])
    return fn


def _cumsum_body(lf_ref, col_ref, kd_ref, carry):
    @pl.when(pl.program_id(1) == 0)
    def _():
        carry[...] = jnp.zeros_like(carry)

    cum = _cumsum_rows(lf_ref[...]) + carry[...]
    carry[...] = cum[cum.shape[0] - 1:, :]
    c2 = cum * LOG2E
    col_ref[...] = c2
    terms = [term.astype(F32) for term in _split3(-c2)]
    lane = lax.broadcasted_iota(jnp.int32, c2.shape, 1)
    for h in range(kd_ref.shape[0]):
        tile = jnp.zeros_like(c2)
        for i, term in enumerate(terms):
            tile = jnp.where(lane == i, term[:, h:h + 1], tile)
        kd_ref[h] = tile.astype(BF16)


def _seq_cumsum(logf, bsz, seq, heads, *, ts):
    return pl.pallas_call(
        _cumsum_body,
        grid=(bsz, seq // ts),
        in_specs=[pl.BlockSpec((None, ts, LANES), lambda b, s: (b, s, 0))],
        out_specs=[pl.BlockSpec((None, ts, LANES), lambda b, s: (b, s, 0)),
                   pl.BlockSpec((None, heads, ts, LANES), lambda b, s: (b, 0, s, 0))],
        out_shape=[jax.ShapeDtypeStruct((bsz, seq, LANES), F32),
                   jax.ShapeDtypeStruct((bsz, heads, seq, LANES), BF16)],
        scratch_shapes=[pltpu.VMEM((1, LANES), F32)],
        compiler_params=_params("parallel", "arbitrary"),
        name="forget_cumsum",
    )(logf.reshape(bsz, seq, LANES))


def _gla_body(q_ref, k_ref, la_ref, v_ref, g_ref, gain_ref, o_ref, st_ref, *, n_chunk, hv):
    @pl.when(pl.program_id(1) == 0)
    def _():
        st_ref[...] = jnp.zeros_like(st_ref)

    gain = gain_ref[...]
    for c in range(n_chunk):
        rows = pl.ds(c * CHUNK, CHUNK)
        cum = _cumsum_rows(la_ref[rows, :])
        total = cum[CHUNK - 1:, :]
        k_dec = (k_ref[rows, :] * jnp.exp(total - cum)).astype(BF16)
        a_chunk = jnp.exp(total)
        for h in range(GLA_HEADS):
            ks = slice(h * GLA_HK_PAD, (h + 1) * GLA_HK_PAD)
            vs = slice(h * hv, (h + 1) * hv)
            upd = lax.dot_general(v_ref[rows, vs], k_dec[:, ks],
                                  (((0,), (0,)), ((), ())), preferred_element_type=F32)
            st = a_chunk[:, ks] * st_ref[h] + upd
            st_ref[h] = st
            o = lax.dot_general(q_ref[rows, ks], st.astype(BF16),
                                (((1,), (1,)), ((), ())), preferred_element_type=F32)
            o = _rms(o, gain) * _silu(g_ref[rows, vs].astype(F32))
            o_ref[rows, vs] = o.astype(BF16)


def _gla(q, k, la, v, g, gain, bsz, seq, *, n_chunk):
    hv = v.shape[1] // GLA_HEADS
    kw = q.shape[1]
    tc = n_chunk * CHUNK
    nt = seq // tc

    def spec(width):
        return pl.BlockSpec((tc, width), lambda b, c: (b * nt + c, 0))

    return pl.pallas_call(
        functools.partial(_gla_body, n_chunk=n_chunk, hv=hv),
        grid=(bsz, nt),
        in_specs=[spec(kw), spec(kw), spec(kw), spec(v.shape[1]), spec(v.shape[1]),
                  pl.BlockSpec((1, hv), lambda b, c: (0, 0))],
        out_specs=spec(v.shape[1]),
        out_shape=jax.ShapeDtypeStruct(v.shape, BF16),
        scratch_shapes=[pltpu.VMEM((GLA_HEADS, hv, GLA_HK_PAD), F32)],
        compiler_params=_params("parallel", "arbitrary"),
        name="gla",
    )(q, k, la, v, g, gain.reshape(1, hv))


def _fox_body(qi_tab, ki_tab, q_ref, k_ref, v_ref, vprev_ref, kd_ref, ccol_ref, g_ref, o_ref,
              qa_scr, ka_scr, p_scr, m_scr, l_scr, acc_scr, cc_scr, *, t, rq, hp):
    step = pl.program_id(2)
    head0 = pl.program_id(1) * hp
    qi = qi_tab[step]
    ki = ki_tab[step]

    @pl.when(ki == 0)
    def _():
        m_scr[...] = jnp.full_like(m_scr, -jnp.inf)
        l_scr[...] = jnp.zeros_like(l_scr)
        acc_scr[...] = jnp.zeros_like(acc_scr)
        blk = ccol_ref[...]
        lane = lax.broadcasted_iota(jnp.int32, blk.shape, 1)
        for j in range(hp):
            cc_scr[j] = jnp.sum(jnp.where(lane == head0 + j, blk, 0.0), axis=1, keepdims=True)
            qa_scr[j, :, :HEAD_DIM] = q_ref[:, j * HEAD_DIM:(j + 1) * HEAD_DIM]
            qa_scr[j, :, HEAD_DIM:] = (lane < DECAY_TERMS).astype(BF16)

    for j in range(hp):
        ka_scr[j, :, :HEAD_DIM] = k_ref[:, j * HEAD_DIM:(j + 1) * HEAD_DIM]
        ka_scr[j, :, HEAD_DIM:] = kd_ref[j]

    def update(j, r, n_cols, masked, pending):
        rows = pl.ds(r * rq, rq)
        u = lax.dot_general(qa_scr[j, rows, :], ka_scr[j, :n_cols, :], (((1,), (1,)), ((), ())),
                            preferred_element_type=F32)
        if masked:
            tail = u[:, n_cols - rq:]
            row = lax.broadcasted_iota(jnp.int32, tail.shape, 0)
            col = lax.broadcasted_iota(jnp.int32, tail.shape, 1)
            tail = jnp.where(col <= row, tail, -jnp.inf)
            u = tail if n_cols == rq else jnp.concatenate([u[:, :n_cols - rq], tail], axis=1)
        cc = cc_scr[j, rows, :]
        m_prev = m_scr[j, rows, :]
        m_new = jnp.maximum(m_prev, jnp.max(u, axis=1, keepdims=True) + cc)
        p = jnp.exp2(u + (cc - m_new))
        alpha = jnp.exp2(m_prev - m_new)
        l_scr[j, rows, :] = alpha * l_scr[j, rows, :] + jnp.sum(p, axis=1, keepdims=True)
        m_scr[j, rows, :] = m_new
        acc = acc_scr[j, rows, :]
        if pending:
            acc = acc + jnp.dot(p_scr[j, rows, :], vprev_ref[:, j * HEAD_DIM:(j + 1) * HEAD_DIM],
                                preferred_element_type=F32)
        acc_scr[j, rows, :] = alpha * acc
        p_scr[j, rows, :n_cols] = p.astype(BF16)

    n_sub = t // rq

    def tile(masked, pending):
        for r in range(n_sub):
            for j in range(hp):
                update(j, r, (r + 1) * rq if masked else t, masked, pending)

    for diag in (False, True):
        for pending in (False, True):
            first = (qi == 0) if diag else (ki == 0)

            @pl.when(((ki == qi) if diag else (ki < qi)) & (~first if pending else first))
            def _(diag=diag, pending=pending):
                tile(diag, pending)

    @pl.when(ki == qi)
    def _():
        for j in range(hp):
            hs = slice(j * HEAD_DIM, (j + 1) * HEAD_DIM)
            for r in range(n_sub):
                rows = pl.ds(r * rq, rq)
                n_cols = (r + 1) * rq
                acc_scr[j, rows, :] += jnp.dot(p_scr[j, rows, :n_cols], v_ref[:n_cols, hs],
                                               preferred_element_type=F32)
            o = acc_scr[j] / l_scr[j]
            o_ref[:, hs] = (o * jax.nn.sigmoid(g_ref[:, hs].astype(F32))).astype(BF16)


def _fox(q, k, v, kd, c2, g, *, t, rq, hp):
    bsz, seq, width = q.shape
    heads = width // HEAD_DIM
    hw = hp * HEAD_DIM
    pairs = [(qi, ki) for qi in range(seq // t) for ki in range(qi + 1)]
    qi_tab = jnp.asarray(np.array([p[0] for p in pairs], np.int32))
    ki_tab = jnp.asarray(np.array([p[1] for p in pairs], np.int32))
    q_spec = pl.BlockSpec((None, t, hw), lambda b, h, s, qt, kt: (b, qt[s], h))
    kv_spec = pl.BlockSpec((None, t, hw), lambda b, h, s, qt, kt: (b, kt[s], h))
    vprev_spec = pl.BlockSpec((None, t, hw),
                              lambda b, h, s, qt, kt: (b, jnp.maximum(kt[s] - 1, 0), h))
    grid_spec = pltpu.PrefetchScalarGridSpec(
        num_scalar_prefetch=2,
        grid=(bsz, heads // hp, len(pairs)),
        in_specs=[
            q_spec, kv_spec, kv_spec, vprev_spec,
            pl.BlockSpec((None, hp, t, LANES), lambda b, h, s, qt, kt: (b, h, kt[s], 0)),
            pl.BlockSpec((None, t, LANES), lambda b, h, s, qt, kt: (b, qt[s], 0)),
            q_spec,
        ],
        out_specs=q_spec,
        scratch_shapes=[pltpu.VMEM((hp, t, 2 * HEAD_DIM), BF16),
                        pltpu.VMEM((hp, t, 2 * HEAD_DIM), BF16),
                        pltpu.VMEM((hp, t, t), BF16),
                        pltpu.VMEM((hp, t, 1), F32), pltpu.VMEM((hp, t, 1), F32),
                        pltpu.VMEM((hp, t, HEAD_DIM), F32), pltpu.VMEM((hp, t, 1), F32)],
    )
    return pl.pallas_call(
        functools.partial(_fox_body, t=t, rq=rq, hp=hp),
        grid_spec=grid_spec,
        out_shape=jax.ShapeDtypeStruct(q.shape, BF16),
        compiler_params=_params("parallel", "parallel", "arbitrary"),
        name="fox",
    )(qi_tab, ki_tab, q, k, v, v, kd, c2, g)


def _mem_attn_body(q_ref, mk_ref, mv_ref, o_ref):
    for h in range(MEM_HEADS):
        hs = slice(h * HEAD_DIM, (h + 1) * HEAD_DIM)
        s = lax.dot_general(q_ref[:, hs], mk_ref[:, hs], (((1,), (1,)), ((), ())),
                            preferred_element_type=F32)
        e = jnp.exp(s - jnp.max(s, axis=1, keepdims=True))
        p = e / jnp.sum(e, axis=1, keepdims=True)
        o_ref[:, hs] = jnp.dot(p.astype(BF16), mv_ref[:, hs],
                               preferred_element_type=F32).astype(BF16)


def _mem_attn(qm, mk, mv, bsz, seq, *, tm):
    n_mem = mk.shape[0] // bsz
    nt = seq // tm
    return pl.pallas_call(
        _mem_attn_body,
        grid=(bsz, nt),
        in_specs=[pl.BlockSpec((tm, MEM_W), lambda b, i: (b * nt + i, 0)),
                  pl.BlockSpec((n_mem, MEM_W), lambda b, i: (b, 0)),
                  pl.BlockSpec((n_mem, MEM_W), lambda b, i: (b, 0))],
        out_specs=pl.BlockSpec((tm, MEM_W), lambda b, i: (b * nt + i, 0)),
        out_shape=jax.ShapeDtypeStruct(qm.shape, BF16),
        compiler_params=_params("parallel", "arbitrary"),
        name="mem_attn",
    )(qm, mk, mv)


def _out_proj_body(x_ref, o_ref, mo_ref, wo_ref, wm_ref, y_ref):
    y = x_ref[...] + jnp.dot(o_ref[...], wo_ref[...], preferred_element_type=F32)
    y_ref[...] = y + jnp.dot(mo_ref[...], wm_ref[...], preferred_element_type=F32)


def _out_proj(x, o, mo, w_o, w_m, *, tm):
    n, d = x.shape
    return pl.pallas_call(
        _out_proj_body,
        grid=(n // tm,),
        in_specs=[pl.BlockSpec((tm, d), lambda i: (i, 0)),
                  pl.BlockSpec((tm, o.shape[1]), lambda i: (i, 0)),
                  pl.BlockSpec((tm, mo.shape[1]), lambda i: (i, 0)),
                  pl.BlockSpec(w_o.shape, lambda i: (0, 0)),
                  pl.BlockSpec(w_m.shape, lambda i: (0, 0))],
        out_specs=pl.BlockSpec((tm, d), lambda i: (i, 0)),
        out_shape=jax.ShapeDtypeStruct((n, d), F32),
        compiler_params=_params("parallel"),
        name="out_proj",
    )(x, o, mo, w_o, w_m)


def _pad_cols(w, width):
    return jnp.pad(w, ((0, 0), (0, width - w.shape[1])))


def _pad_gla_heads(w, hk):
    lead = w.shape[:-1]
    w = w.reshape(lead + (GLA_HEADS, hk))
    w = jnp.pad(w, [(0, 0)] * len(lead) + [(0, 0), (0, GLA_HK_PAD - hk)])
    return w.reshape(lead + (GLA_HEADS * GLA_HK_PAD,))


def _forward(x, mem, ffn_norm, ffn_w1, ffn_w3, ffn_w2, mix_norm, mem_norm, w_mem_kv,
             mem_q_norm, mem_k_norm, w_out, a_w_in, a_w_gate_up, a_b_gate, a_out_norm,
             b_w_in, b_q_norm, kv_norm, w_kv, b_f, k_norm, *, tiles):
    bsz, seq, d = x.shape
    n = bsz * seq
    depth = ffn_norm.shape[0]
    n_a = a_w_in.shape[0]
    main_w = d - MEM_W
    gla_key = main_w // 2
    hk = gla_key // GLA_HEADS
    fox_heads = main_w // HEAD_DIM
    tn = tiles["proj_tn"]
    attn_scale = HEAD_DIM ** -0.5

    w1 = ffn_w1.astype(BF16)
    w3 = ffn_w3.astype(BF16)
    w2 = ffn_w2.astype(BF16)
    x2 = x.reshape(n, d)
    mem2 = mem.reshape(bsz * mem.shape[1], d)

    def ffn(x2, l, s):
        return _ffn(x2, ffn_norm[l, s], w1, w3, w2, l, s,
                    tm=tiles["ffn_tm"], tf=tiles["ffn_tf"])

    k_sh = v_sh = c2 = k_decay = None
    for l in range(depth):
        if l == n_a:
            w = jnp.concatenate([w_kv[:, :2 * main_w], _pad_cols(w_kv[:, 2 * main_w:], tn)],
                                axis=1).astype(BF16)
            bias = _pad_cols(b_f.reshape(1, fox_heads), LANES)
            k_sh, v_sh, logf = _norm_proj(
                x2, kv_norm, w,
                [(main_w // tn, tn, BF16, _head_norm_ep(0, 1.0)),
                 (main_w // tn, tn, BF16, _plain_ep),
                 (1, LANES, F32, _forget_gate_ep(1))],
                [k_norm.reshape(1, HEAD_DIM), bias],
                tm=tiles["proj_tm"], tn=tn, name="kv_proj")
            c2, k_decay = _seq_cumsum(logf, bsz, seq, fox_heads, ts=tiles["cum_ts"])

        x2 = ffn(x2, l, 0)

        mk, mv = _norm_proj(
            mem2, mem_norm[l], w_mem_kv[l].astype(BF16),
            [(MEM_W // tn, tn, BF16, _head_norm_ep(0, 1.0)),
             (MEM_W // tn, tn, BF16, _plain_ep)],
            [mem_k_norm[l].reshape(1, HEAD_DIM)],
            tm=tiles["mem_tm"], tn=tn, name="mem_kv_proj")
        qm_ep = _head_norm_ep(0, attn_scale)

        if l < n_a:
            wi = a_w_in[l]
            c = np.cumsum([0, gla_key, gla_key, main_w, GLA_GATE_RANK, main_w, MEM_W])
            w = jnp.concatenate([
                _pad_gla_heads(wi[:, c[0]:c[1]], hk),
                _pad_gla_heads(wi[:, c[1]:c[2]], hk),
                wi[:, c[2]:c[3]],
                wi[:, c[4]:c[5]],
                wi[:, c[5]:c[6]],
                _pad_cols(wi[:, c[3]:c[4]], tn),
            ], axis=1).astype(BF16)
            kw = GLA_HEADS * GLA_HK_PAD
            w_up = jnp.pad(_pad_gla_heads(a_w_gate_up[l], hk),
                           ((0, tn - GLA_GATE_RANK), (0, 0))).astype(BF16)
            b_gate = _pad_gla_heads(a_b_gate[l].reshape(1, gla_key), hk)
            q, k, v, g, qm, la = _norm_proj(
                x2, mix_norm[l], w,
                [(kw // tn, tn, BF16, _scale_ep(hk ** -0.5)),
                 (kw // tn, tn, F32, _plain_ep),
                 (main_w // tn, tn, BF16, _plain_ep),
                 (main_w // tn, tn, BF16, _plain_ep),
                 (MEM_W // tn, tn, BF16, qm_ep),
                 (1, kw, F32, _gla_gate_ep(1, 2))],
                [mem_q_norm[l].reshape(1, HEAD_DIM), w_up, b_gate],
                tm=tiles["proj_tm"], tn=tn, name="gla_in_proj")
            o = _gla(q, k, la, v, g, a_out_norm[l], bsz, seq, n_chunk=tiles["gla_chunks"])
        else:
            jl = l - n_a
            q, g, qm = _norm_proj(
                x2, mix_norm[l], b_w_in[jl].astype(BF16),
                [(main_w // tn, tn, BF16, _head_norm_ep(0, attn_scale * LOG2E)),
                 (main_w // tn, tn, BF16, _plain_ep),
                 (MEM_W // tn, tn, BF16, _head_norm_ep(1, attn_scale))],
                [b_q_norm[jl].reshape(1, HEAD_DIM), mem_q_norm[l].reshape(1, HEAD_DIM)],
                tm=tiles["proj_tm"], tn=tn, name="fox_in_proj")
            o = _fox(q.reshape(bsz, seq, main_w), k_sh.reshape(bsz, seq, main_w),
                     v_sh.reshape(bsz, seq, main_w), k_decay, c2,
                     g.reshape(bsz, seq, main_w), t=tiles["fox_t"], rq=tiles["fox_rq"], hp=tiles["fox_hp"])
            o = o.reshape(n, main_w)

        mo = _mem_attn(qm, mk, mv, bsz, seq, tm=tiles["attn_tm"])
        wo = w_out[l].astype(BF16)
        x2 = _out_proj(x2, o, mo, wo[:main_w], wo[main_w:], tm=tiles["out_tm"])
        x2 = ffn(x2, l, 1)
    return x2.reshape(bsz, seq, d)


_TILES = dict(ffn_tm=1024, ffn_tf=512, proj_tm=1024, proj_tn=512, mem_tm=512, cum_ts=512,
              gla_chunks=8, fox_t=1024, fox_rq=256, fox_hp=2, attn_tm=1024, out_tm=512)


def kernel(x, mem, ffn_norm, ffn_w1, ffn_w3, ffn_w2, mix_norm, mem_norm, w_mem_kv, mem_q_norm, mem_k_norm, w_out, a_w_in, a_w_gate_up, a_b_gate, a_out_norm, b_w_in, b_q_norm, kv_norm, w_kv, b_f, k_norm):
    return _forward(x, mem, ffn_norm, ffn_w1, ffn_w3, ffn_w2, mix_norm, mem_norm, w_mem_kv,
                    mem_q_norm, mem_k_norm, w_out, a_w_in, a_w_gate_up, a_b_gate, a_out_norm,
                    b_w_in, b_q_norm, kv_norm, w_kv, b_f, k_norm, tiles=_TILES)
```

```python
import functools

import numpy as np
import jax
import jax.numpy as jnp
from jax import lax
from jax.experimental import pallas as pl
from jax.experimental.pallas import tpu as pltpu

F32 = jnp.float32
BF16 = jnp.bfloat16

CHUNK = 64
HEAD_DIM = 128
MEM_HEADS = 4
MEM_W = MEM_HEADS * HEAD_DIM
GLA_HEADS = 4
GLA_GATE_RANK = 16
GLA_GATE_NORMALIZER = 16.0
EPS = 1e-6
LOG2E = 1.4426950408889634
DECAY_TERMS = 3

LANES = 128
V7X_VMEM_LIMIT_BYTES = 56 * 1024 * 1024

GLA_HK_PAD = 256


def _rms(x, gain):
    return x * lax.rsqrt(jnp.mean(x * x, axis=-1, keepdims=True) + EPS) * gain


def _log_sigmoid(x):
    return jnp.minimum(x, 0.0) - jnp.log1p(jnp.exp(-jnp.abs(x)))


def _silu(x):
    return x * jax.nn.sigmoid(x)


def _split3(x):
    hi = x.astype(BF16)
    r1 = x - hi.astype(F32)
    mid = r1.astype(BF16)
    lo = (r1 - mid.astype(F32)).astype(BF16)
    return hi, mid, lo


def _cumsum_rows(x):
    n = x.shape[0]
    row = lax.broadcasted_iota(jnp.int32, (n, n), 0)
    col = lax.broadcasted_iota(jnp.int32, (n, n), 1)
    tri = (row >= col).astype(BF16)
    return sum(jnp.dot(tri, term, preferred_element_type=F32) for term in _split3(x))


def _params(*semantics):
    return pltpu.CompilerParams(dimension_semantics=semantics,
                                vmem_limit_bytes=V7X_VMEM_LIMIT_BYTES)


def _ffn_body(x_ref, g_ref, w1_ref, w3_ref, w2_ref, o_ref, h_scr):
    def ff_tile(h):
        a = jnp.dot(h, w1_ref[...], preferred_element_type=F32)
        b = jnp.dot(h, w3_ref[...], preferred_element_type=F32)
        act = (_silu(a) * b * 0.5).astype(BF16)
        return jnp.dot(act, w2_ref[...], preferred_element_type=F32)

    @pl.when(pl.program_id(1) == 0)
    def _():
        x = x_ref[...]
        h = _rms(x, g_ref[...]).astype(BF16)
        h_scr[...] = h
        o_ref[...] = x + ff_tile(h)

    @pl.when(pl.program_id(1) > 0)
    def _():
        o_ref[...] += ff_tile(h_scr[...])


def _ffn(x, gain, w1, w3, w2, layer, half, *, tm, tf):
    n, d = x.shape
    f = w1.shape[-1]
    return pl.pallas_call(
        _ffn_body,
        grid=(n // tm, f // tf),
        in_specs=[
            pl.BlockSpec((tm, d), lambda i, j: (i, 0)),
            pl.BlockSpec((1, d), lambda i, j: (0, 0)),
            pl.BlockSpec((None, None, d, tf), lambda i, j: (layer, half, 0, j)),
            pl.BlockSpec((None, None, d, tf), lambda i, j: (layer, half, 0, j)),
            pl.BlockSpec((None, None, tf, d), lambda i, j: (layer, half, j, 0)),
        ],
        out_specs=pl.BlockSpec((tm, d), lambda i, j: (i, 0)),
        out_shape=jax.ShapeDtypeStruct((n, d), F32),
        scratch_shapes=[pltpu.VMEM((tm, d), BF16)],
        compiler_params=_params("parallel", "arbitrary"),
        name="ffn",
    )(x, gain.reshape(1, d), w1, w3, w2)


def _norm_proj_body(*refs, groups, n_extra):
    x_ref, g_ref, w_ref = refs[:3]
    extra = refs[3:3 + n_extra]
    outs = refs[3 + n_extra:3 + n_extra + len(groups)]
    h_scr = refs[-1]
    j = pl.program_id(1)

    def tile(h, fn, o_ref):
        acc = jnp.dot(h, w_ref[...], preferred_element_type=F32)
        o_ref[...] = fn(acc, extra).astype(o_ref.dtype)

    @pl.when(j == 0)
    def _():
        h = _rms(x_ref[...], g_ref[...]).astype(BF16)
        h_scr[...] = h
        tile(h, groups[0][3], outs[0])

    start = 0
    for (n_tiles, _, _, fn), o_ref in zip(groups, outs):
        @pl.when((j >= max(start, 1)) & (j < start + n_tiles))
        def _(fn=fn, o_ref=o_ref):
            tile(h_scr[...], fn, o_ref)
        start += n_tiles


def _norm_proj(x, gain, w, groups, extras, *, tm, tn, name):
    n, d = x.shape
    assert w.shape[1] == tn * sum(g[0] for g in groups)
    in_specs = [
        pl.BlockSpec((tm, d), lambda i, j: (i, 0)),
        pl.BlockSpec((1, d), lambda i, j: (0, 0)),
        pl.BlockSpec((d, tn), lambda i, j: (0, j)),
    ] + [pl.BlockSpec(e.shape, lambda i, j: (0, 0)) for e in extras]
    out_specs, out_shapes = [], []
    start = 0
    for n_tiles, width, dtype, _ in groups:
        out_specs.append(pl.BlockSpec(
            (tm, width),
            lambda i, j, s=start, c=n_tiles: (i, jnp.clip(j - s, 0, c - 1))))
        out_shapes.append(jax.ShapeDtypeStruct((n, width * n_tiles), dtype))
        start += n_tiles
    return pl.pallas_call(
        functools.partial(_norm_proj_body, groups=groups, n_extra=len(extras)),
        grid=(n // tm, start),
        in_specs=in_specs,
        out_specs=out_specs,
        out_shape=out_shapes,
        scratch_shapes=[pltpu.VMEM((tm, d), BF16)],
        compiler_params=_params("parallel", "arbitrary"),
        name=name,
    )(x, gain.reshape(1, d), w, *extras)


def _scale_ep(scale):
    return lambda acc, extra: acc * scale


def _plain_ep(acc, extra):
    return acc


def _head_norm_ep(gain_idx, scale):
    def fn(acc, extra):
        gain = extra[gain_idx][...]
        parts = []
        for h in range(acc.shape[1] // HEAD_DIM):
            a = acc[:, h * HEAD_DIM:(h + 1) * HEAD_DIM]
            parts.append(_rms(a, gain) * scale)
        return jnp.concatenate(parts, axis=-1)
    return fn


def _gla_gate_ep(w_up_idx, bias_idx):
    def fn(acc, extra):
        z = jnp.dot(acc.astype(BF16), extra[w_up_idx][...], preferred_element_type=F32)
        return _log_sigmoid(z + extra[bias_idx][...]) / GLA_GATE_NORMALIZER
    return fn


def _forget_gate_ep(bias_idx):
    def fn(acc, extra):
        return _log_sigmoid(acc[:, :LANES] + extra[bias_idx][...])
    return fn


def _cumsum_body(lf_ref, col_ref, kd_ref, carry):
    @pl.when(pl.program_id(1) == 0)
    def _():
        carry[...] = jnp.zeros_like(carry)

    cum = _cumsum_rows(lf_ref[...]) + carry[...]
    carry[...] = cum[cum.shape[0] - 1:, :]
    c2 = cum * LOG2E
    col_ref[...] = c2
    terms = [term.astype(F32) for term in _split3(-c2)]
    lane = lax.broadcasted_iota(jnp.int32, c2.shape, 1)
    for h in range(kd_ref.shape[0]):
        tile = jnp.zeros_like(c2)
        for i, term in enumerate(terms):
            tile = jnp.where(lane == i, term[:, h:h + 1], tile)
        kd_ref[h] = tile.astype(BF16)


def _seq_cumsum(logf, bsz, seq, heads, *, ts):
    return pl.pallas_call(
        _cumsum_body,
        grid=(bsz, seq // ts),
        in_specs=[pl.BlockSpec((None, ts, LANES), lambda b, s: (b, s, 0))],
        out_specs=[pl.BlockSpec((None, ts, LANES), lambda b, s: (b, s, 0)),
                   pl.BlockSpec((None, heads, ts, LANES), lambda b, s: (b, 0, s, 0))],
        out_shape=[jax.ShapeDtypeStruct((bsz, seq, LANES), F32),
                   jax.ShapeDtypeStruct((bsz, heads, seq, LANES), BF16)],
        scratch_shapes=[pltpu.VMEM((1, LANES), F32)],
        compiler_params=_params("parallel", "arbitrary"),
        name="forget_cumsum",
    )(logf.reshape(bsz, seq, LANES))


def _gla_body(q_ref, k_ref, la_ref, v_ref, g_ref, gain_ref, o_ref, st_ref, *, n_chunk, hv):
    @pl.when(pl.program_id(1) == 0)
    def _():
        st_ref[...] = jnp.zeros_like(st_ref)

    gain = gain_ref[...]
    for c in range(n_chunk):
        rows = pl.ds(c * CHUNK, CHUNK)
        cum = _cumsum_rows(la_ref[rows, :])
        total = cum[CHUNK - 1:, :]
        k_dec = (k_ref[rows, :] * jnp.exp(total - cum)).astype(BF16)
        a_chunk = jnp.exp(total)
        for h in range(GLA_HEADS):
            ks = slice(h * GLA_HK_PAD, (h + 1) * GLA_HK_PAD)
            vs = slice(h * hv, (h + 1) * hv)
            upd = lax.dot_general(v_ref[rows, vs], k_dec[:, ks],
                                  (((0,), (0,)), ((), ())), preferred_element_type=F32)
            st = a_chunk[:, ks] * st_ref[h] + upd
            st_ref[h] = st
            o = lax.dot_general(q_ref[rows, ks], st.astype(BF16),
                                (((1,), (1,)), ((), ())), preferred_element_type=F32)
            o = _rms(o, gain) * _silu(g_ref[rows, vs].astype(F32))
            o_ref[rows, vs] = o.astype(BF16)


def _gla(q, k, la, v, g, gain, bsz, seq, *, n_chunk):
    hv = v.shape[1] // GLA_HEADS
    kw = q.shape[1]
    tc = n_chunk * CHUNK
    nt = seq // tc

    def spec(width):
        return pl.BlockSpec((tc, width), lambda b, c: (b * nt + c, 0))

    return pl.pallas_call(
        functools.partial(_gla_body, n_chunk=n_chunk, hv=hv),
        grid=(bsz, nt),
        in_specs=[spec(kw), spec(kw), spec(kw), spec(v.shape[1]), spec(v.shape[1]),
                  pl.BlockSpec((1, hv), lambda b, c: (0, 0))],
        out_specs=spec(v.shape[1]),
        out_shape=jax.ShapeDtypeStruct(v.shape, BF16),
        scratch_shapes=[pltpu.VMEM((GLA_HEADS, hv, GLA_HK_PAD), F32)],
        compiler_params=_params("parallel", "arbitrary"),
        name="gla",
    )(q, k, la, v, g, gain.reshape(1, hv))


def _fox_body(qi_tab, ki_tab, q_ref, k_ref, v_ref, vprev_ref, kd_ref, ccol_ref, g_ref, o_ref,
              qa_scr, ka_scr, p_scr, m_scr, l_scr, acc_scr, cc_scr, *, t, rq, hp):
    step = pl.program_id(2)
    head0 = pl.program_id(1) * hp
    qi = qi_tab[step]
    ki = ki_tab[step]

    def start_query_tile():
        m_scr[...] = jnp.full_like(m_scr, -jnp.inf)
        l_scr[...] = jnp.zeros_like(l_scr)
        acc_scr[...] = jnp.zeros_like(acc_scr)
        blk = ccol_ref[...]
        lane = lax.broadcasted_iota(jnp.int32, blk.shape, 1)
        for j in range(hp):
            cc_scr[j] = jnp.sum(jnp.where(lane == head0 + j, blk, 0.0), axis=1, keepdims=True)
            qa_scr[j, :, :HEAD_DIM] = q_ref[:, j * HEAD_DIM:(j + 1) * HEAD_DIM]
            qa_scr[j, :, HEAD_DIM:] = (lane < DECAY_TERMS).astype(BF16)

    def load_keys():
        for j in range(hp):
            ka_scr[j, :, :HEAD_DIM] = k_ref[:, j * HEAD_DIM:(j + 1) * HEAD_DIM]
            ka_scr[j, :, HEAD_DIM:] = kd_ref[j]

    def update(j, r, n_cols, masked, pending):
        rows = pl.ds(r * rq, rq)
        u = lax.dot_general(qa_scr[j, rows, :], ka_scr[j, :n_cols, :], (((1,), (1,)), ((), ())),
                            preferred_element_type=F32)
        if masked:
            tail = u[:, n_cols - rq:]
            row = lax.broadcasted_iota(jnp.int32, tail.shape, 0)
            col = lax.broadcasted_iota(jnp.int32, tail.shape, 1)
            tail = jnp.where(col <= row, tail, -jnp.inf)
            u = tail if n_cols == rq else jnp.concatenate([u[:, :n_cols - rq], tail], axis=1)
        cc = cc_scr[j, rows, :]
        m_prev = m_scr[j, rows, :]
        m_new = jnp.maximum(m_prev, jnp.max(u, axis=1, keepdims=True) + cc)
        p = jnp.exp2(u + (cc - m_new))
        alpha = jnp.exp2(m_prev - m_new)
        l_scr[j, rows, :] = alpha * l_scr[j, rows, :] + jnp.sum(p, axis=1, keepdims=True)
        m_scr[j, rows, :] = m_new
        acc = acc_scr[j, rows, :]
        if pending:
            acc = acc + jnp.dot(p_scr[j, rows, :], vprev_ref[:, j * HEAD_DIM:(j + 1) * HEAD_DIM],
                                preferred_element_type=F32)
        acc_scr[j, rows, :] = alpha * acc
        p_scr[j, rows, :n_cols] = p.astype(BF16)

    n_sub = t // rq

    def tile(masked, pending):
        for r in range(n_sub):
            for j in range(hp):
                update(j, r, (r + 1) * rq if masked else t, masked, pending)

    for diag in (False, True):
        for pending in (False, True):
            first = ki == 0

            @pl.when(((ki == qi) if diag else (ki < qi)) & (~first if pending else first))
            def _(diag=diag, pending=pending):
                if not pending:
                    start_query_tile()
                load_keys()
                tile(diag, pending)

    @pl.when(ki == qi)
    def _():
        for j in range(hp):
            hs = slice(j * HEAD_DIM, (j + 1) * HEAD_DIM)
            for r in range(n_sub):
                rows = pl.ds(r * rq, rq)
                n_cols = (r + 1) * rq
                acc_scr[j, rows, :] += jnp.dot(p_scr[j, rows, :n_cols], v_ref[:n_cols, hs],
                                               preferred_element_type=F32)
            o = acc_scr[j] / l_scr[j]
            o_ref[:, hs] = (o * jax.nn.sigmoid(g_ref[:, hs].astype(F32))).astype(BF16)


def _fox(q, k, v, kd, c2, g, *, t, rq, hp):
    bsz, seq, width = q.shape
    heads = width // HEAD_DIM
    hw = hp * HEAD_DIM
    pairs = [(qi, ki) for qi in range(seq // t) for ki in range(qi + 1)]
    qi_tab = jnp.asarray(np.array([p[0] for p in pairs], np.int32))
    ki_tab = jnp.asarray(np.array([p[1] for p in pairs], np.int32))
    q_spec = pl.BlockSpec((None, t, hw), lambda b, h, s, qt, kt: (b, qt[s], h))
    kv_spec = pl.BlockSpec((None, t, hw), lambda b, h, s, qt, kt: (b, kt[s], h))
    vprev_spec = pl.BlockSpec((None, t, hw),
                              lambda b, h, s, qt, kt: (b, jnp.maximum(kt[s] - 1, 0), h))
    grid_spec = pltpu.PrefetchScalarGridSpec(
        num_scalar_prefetch=2,
        grid=(bsz, heads // hp, len(pairs)),
        in_specs=[
            q_spec, kv_spec, kv_spec, vprev_spec,
            pl.BlockSpec((None, hp, t, LANES), lambda b, h, s, qt, kt: (b, h, kt[s], 0)),
            pl.BlockSpec((None, t, LANES), lambda b, h, s, qt, kt: (b, qt[s], 0)),
            q_spec,
        ],
        out_specs=q_spec,
        scratch_shapes=[pltpu.VMEM((hp, t, 2 * HEAD_DIM), BF16),
                        pltpu.VMEM((hp, t, 2 * HEAD_DIM), BF16),
                        pltpu.VMEM((hp, t, t), BF16),
                        pltpu.VMEM((hp, t, 1), F32), pltpu.VMEM((hp, t, 1), F32),
                        pltpu.VMEM((hp, t, HEAD_DIM), F32), pltpu.VMEM((hp, t, 1), F32)],
    )
    return pl.pallas_call(
        functools.partial(_fox_body, t=t, rq=rq, hp=hp),
        grid_spec=grid_spec,
        out_shape=jax.ShapeDtypeStruct(q.shape, BF16),
        compiler_params=_params("parallel", "parallel", "arbitrary"),
        name="fox",
    )(qi_tab, ki_tab, q, k, v, v, kd, c2, g)


def _mem_attn_body(q_ref, mk_ref, mv_ref, o_ref):
    for h in range(MEM_HEADS):
        hs = slice(h * HEAD_DIM, (h + 1) * HEAD_DIM)
        s = lax.dot_general(q_ref[:, hs], mk_ref[:, hs], (((1,), (1,)), ((), ())),
                            preferred_element_type=F32)
        e = jnp.exp(s - jnp.max(s, axis=1, keepdims=True))
        p = e / jnp.sum(e, axis=1, keepdims=True)
        o_ref[:, hs] = jnp.dot(p.astype(BF16), mv_ref[:, hs],
                               preferred_element_type=F32).astype(BF16)


def _mem_attn(qm, mk, mv, bsz, seq, *, tm):
    n_mem = mk.shape[0] // bsz
    nt = seq // tm
    return pl.pallas_call(
        _mem_attn_body,
        grid=(bsz, nt),
        in_specs=[pl.BlockSpec((tm, MEM_W), lambda b, i: (b * nt + i, 0)),
                  pl.BlockSpec((n_mem, MEM_W), lambda b, i: (b, 0)),
                  pl.BlockSpec((n_mem, MEM_W), lambda b, i: (b, 0))],
        out_specs=pl.BlockSpec((tm, MEM_W), lambda b, i: (b * nt + i, 0)),
        out_shape=jax.ShapeDtypeStruct(qm.shape, BF16),
        compiler_params=_params("parallel", "arbitrary"),
        name="mem_attn",
    )(qm, mk, mv)


def _out_proj_body(x_ref, o_ref, mo_ref, wo_ref, wm_ref, y_ref):
    y = x_ref[...] + jnp.dot(o_ref[...], wo_ref[...], preferred_element_type=F32)
    y_ref[...] = y + jnp.dot(mo_ref[...], wm_ref[...], preferred_element_type=F32)


def _out_proj(x, o, mo, w_o, w_m, *, tm):
    n, d = x.shape
    return pl.pallas_call(
        _out_proj_body,
        grid=(n // tm,),
        in_specs=[pl.BlockSpec((tm, d), lambda i: (i, 0)),
                  pl.BlockSpec((tm, o.shape[1]), lambda i: (i, 0)),
                  pl.BlockSpec((tm, mo.shape[1]), lambda i: (i, 0)),
                  pl.BlockSpec(w_o.shape, lambda i: (0, 0)),
                  pl.BlockSpec(w_m.shape, lambda i: (0, 0))],
        out_specs=pl.BlockSpec((tm, d), lambda i: (i, 0)),
        out_shape=jax.ShapeDtypeStruct((n, d), F32),
        compiler_params=_params("parallel"),
        name="out_proj",
    )(x, o, mo, w_o, w_m)


def _pad_cols(w, width):
    return jnp.pad(w, ((0, 0), (0, width - w.shape[1])))


def _pad_gla_heads(w, hk):
    lead = w.shape[:-1]
    w = w.reshape(lead + (GLA_HEADS, hk))
    w = jnp.pad(w, [(0, 0)] * len(lead) + [(0, 0), (0, GLA_HK_PAD - hk)])
    return w.reshape(lead + (GLA_HEADS * GLA_HK_PAD,))


def _forward(x, mem, ffn_norm, ffn_w1, ffn_w3, ffn_w2, mix_norm, mem_norm, w_mem_kv,
             mem_q_norm, mem_k_norm, w_out, a_w_in, a_w_gate_up, a_b_gate, a_out_norm,
             b_w_in, b_q_norm, kv_norm, w_kv, b_f, k_norm, *, tiles):
    bsz, seq, d = x.shape
    n = bsz * seq
    depth = ffn_norm.shape[0]
    n_a = a_w_in.shape[0]
    main_w = d - MEM_W
    gla_key = main_w // 2
    hk = gla_key // GLA_HEADS
    fox_heads = main_w // HEAD_DIM
    tn = tiles["proj_tn"]
    attn_scale = HEAD_DIM ** -0.5

    w1 = ffn_w1.astype(BF16)
    w3 = ffn_w3.astype(BF16)
    w2 = ffn_w2.astype(BF16)
    x2 = x.reshape(n, d)
    mem2 = mem.reshape(bsz * mem.shape[1], d)

    def ffn(x2, l, s):
        return _ffn(x2, ffn_norm[l, s], w1, w3, w2, l, s,
                    tm=tiles["ffn_tm"], tf=tiles["ffn_tf"])

    k_sh = v_sh = c2 = k_decay = None
    for l in range(depth):
        if l == n_a:
            w = jnp.concatenate([w_kv[:, :2 * main_w], _pad_cols(w_kv[:, 2 * main_w:], tn)],
                                axis=1).astype(BF16)
            bias = _pad_cols(b_f.reshape(1, fox_heads), LANES)
            k_sh, v_sh, logf = _norm_proj(
                x2, kv_norm, w,
                [(main_w // tn, tn, BF16, _head_norm_ep(0, 1.0)),
                 (main_w // tn, tn, BF16, _plain_ep),
                 (1, LANES, F32, _forget_gate_ep(1))],
                [k_norm.reshape(1, HEAD_DIM), bias],
                tm=tiles["proj_tm"], tn=tn, name="kv_proj")
            c2, k_decay = _seq_cumsum(logf, bsz, seq, fox_heads, ts=tiles["cum_ts"])

        x2 = ffn(x2, l, 0)

        mk, mv = _norm_proj(
            mem2, mem_norm[l], w_mem_kv[l].astype(BF16),
            [(MEM_W // tn, tn, BF16, _head_norm_ep(0, 1.0)),
             (MEM_W // tn, tn, BF16, _plain_ep)],
            [mem_k_norm[l].reshape(1, HEAD_DIM)],
            tm=tiles["mem_tm"], tn=tn, name="mem_kv_proj")
        qm_ep = _head_norm_ep(0, attn_scale)

        if l < n_a:
            wi = a_w_in[l]
            c = np.cumsum([0, gla_key, gla_key, main_w, GLA_GATE_RANK, main_w, MEM_W])
            w = jnp.concatenate([
                _pad_gla_heads(wi[:, c[0]:c[1]], hk),
                _pad_gla_heads(wi[:, c[1]:c[2]], hk),
                wi[:, c[2]:c[3]],
                wi[:, c[4]:c[5]],
                wi[:, c[5]:c[6]],
                _pad_cols(wi[:, c[3]:c[4]], tn),
            ], axis=1).astype(BF16)
            kw = GLA_HEADS * GLA_HK_PAD
            w_up = jnp.pad(_pad_gla_heads(a_w_gate_up[l], hk),
                           ((0, tn - GLA_GATE_RANK), (0, 0))).astype(BF16)
            b_gate = _pad_gla_heads(a_b_gate[l].reshape(1, gla_key), hk)
            q, k, v, g, qm, la = _norm_proj(
                x2, mix_norm[l], w,
                [(kw // tn, tn, BF16, _scale_ep(hk ** -0.5)),
                 (kw // tn, tn, F32, _plain_ep),
                 (main_w // tn, tn, BF16, _plain_ep),
                 (main_w // tn, tn, BF16, _plain_ep),
                 (MEM_W // tn, tn, BF16, qm_ep),
                 (1, kw, F32, _gla_gate_ep(1, 2))],
                [mem_q_norm[l].reshape(1, HEAD_DIM), w_up, b_gate],
                tm=tiles["proj_tm"], tn=tn, name="gla_in_proj")
            o = _gla(q, k, la, v, g, a_out_norm[l], bsz, seq, n_chunk=tiles["gla_chunks"])
        else:
            jl = l - n_a
            q, g, qm = _norm_proj(
                x2, mix_norm[l], b_w_in[jl].astype(BF16),
                [(main_w // tn, tn, BF16, _head_norm_ep(0, attn_scale * LOG2E)),
                 (main_w // tn, tn, BF16, _plain_ep),
                 (MEM_W // tn, tn, BF16, _head_norm_ep(1, attn_scale))],
                [b_q_norm[jl].reshape(1, HEAD_DIM), mem_q_norm[l].reshape(1, HEAD_DIM)],
                tm=tiles["proj_tm"], tn=tn, name="fox_in_proj")
            o = _fox(q.reshape(bsz, seq, main_w), k_sh.reshape(bsz, seq, main_w),
                     v_sh.reshape(bsz, seq, main_w), k_decay, c2,
                     g.reshape(bsz, seq, main_w), t=tiles["fox_t"], rq=tiles["fox_rq"], hp=tiles["fox_hp"])
            o = o.reshape(n, main_w)

        mo = _mem_attn(qm, mk, mv, bsz, seq, tm=tiles["attn_tm"])
        wo = w_out[l].astype(BF16)
        x2 = _out_proj(x2, o, mo, wo[:main_w], wo[main_w:], tm=tiles["out_tm"])
        x2 = ffn(x2, l, 1)
    return x2.reshape(bsz, seq, d)


_TILES = dict(ffn_tm=1024, ffn_tf=512, proj_tm=1024, proj_tn=512, mem_tm=512, cum_ts=512,
              gla_chunks=8, fox_t=1024, fox_rq=256, fox_hp=2, attn_tm=1024, out_tm=512)


def kernel(x, mem, ffn_norm, ffn_w1, ffn_w3, ffn_w2, mix_norm, mem_norm, w_mem_kv, mem_q_norm, mem_k_norm, w_out, a_w_in, a_w_gate_up, a_b_gate, a_out_norm, b_w_in, b_q_norm, kv_norm, w_kv, b_f, k_norm):
    return _forward(x, mem, ffn_norm, ffn_w1, ffn_w3, ffn_w2, mix_norm, mem_norm, w_mem_kv,
                    mem_q_norm, mem_k_norm, w_out, a_w_in, a_w_gate_up, a_b_gate, a_out_norm,
                    b_w_in, b_q_norm, kv_norm, w_kv, b_f, k_norm, tiles=_TILES)
```

```python
import functools

import numpy as np
import jax
import jax.numpy as jnp
from jax import lax
from jax.experimental import pallas as pl
from jax.experimental.pallas import tpu as pltpu

F32 = jnp.float32
BF16 = jnp.bfloat16

CHUNK = 64
HEAD_DIM = 128
MEM_HEADS = 4
MEM_W = MEM_HEADS * HEAD_DIM
GLA_HEADS = 4
GLA_GATE_RANK = 16
GLA_GATE_NORMALIZER = 16.0
EPS = 1e-6
LOG2E = 1.4426950408889634
DECAY_TERMS = 3

LANES = 128
V7X_VMEM_LIMIT_BYTES = 56 * 1024 * 1024

GLA_HK_PAD = 256


def _rms(x, gain):
    return x * lax.rsqrt(jnp.mean(x * x, axis=-1, keepdims=True) + EPS) * gain


def _log_sigmoid(x):
    return jnp.minimum(x, 0.0) - jnp.log1p(jnp.exp(-jnp.abs(x)))


def _silu(x):
    return x * jax.nn.sigmoid(x)


def _split3(x):
    hi = x.astype(BF16)
    r1 = x - hi.astype(F32)
    mid = r1.astype(BF16)
    lo = (r1 - mid.astype(F32)).astype(BF16)
    return hi, mid, lo


def _cumsum_rows(x):
    n = x.shape[0]
    row = lax.broadcasted_iota(jnp.int32, (n, n), 0)
    col = lax.broadcasted_iota(jnp.int32, (n, n), 1)
    tri = (row >= col).astype(BF16)
    return sum(jnp.dot(tri, term, preferred_element_type=F32) for term in _split3(x))


def _params(*semantics):
    return pltpu.CompilerParams(dimension_semantics=semantics,
                                vmem_limit_bytes=V7X_VMEM_LIMIT_BYTES)


def _ffn_body(x_ref, g_ref, w1_ref, w3_ref, w2_ref, o_ref, h_scr):
    def ff_tile(h):
        a = jnp.dot(h, w1_ref[...], preferred_element_type=F32)
        b = jnp.dot(h, w3_ref[...], preferred_element_type=F32)
        act = (_silu(a) * b * 0.5).astype(BF16)
        return jnp.dot(act, w2_ref[...], preferred_element_type=F32)

    @pl.when(pl.program_id(1) == 0)
    def _():
        x = x_ref[...]
        h = _rms(x, g_ref[...]).astype(BF16)
        h_scr[...] = h
        o_ref[...] = x + ff_tile(h)

    @pl.when(pl.program_id(1) > 0)
    def _():
        o_ref[...] += ff_tile(h_scr[...])


def _ffn(x, gain, w1, w3, w2, layer, half, *, tm, tf):
    n, d = x.shape
    f = w1.shape[-1]
    return pl.pallas_call(
        _ffn_body,
        grid=(n // tm, f // tf),
        in_specs=[
            pl.BlockSpec((tm, d), lambda i, j: (i, 0)),
            pl.BlockSpec((1, d), lambda i, j: (0, 0)),
            pl.BlockSpec((None, None, d, tf), lambda i, j: (layer, half, 0, j)),
            pl.BlockSpec((None, None, d, tf), lambda i, j: (layer, half, 0, j)),
            pl.BlockSpec((None, None, tf, d), lambda i, j: (layer, half, j, 0)),
        ],
        out_specs=pl.BlockSpec((tm, d), lambda i, j: (i, 0)),
        out_shape=jax.ShapeDtypeStruct((n, d), F32),
        scratch_shapes=[pltpu.VMEM((tm, d), BF16)],
        compiler_params=_params("parallel", "arbitrary"),
        name="ffn",
    )(x, gain.reshape(1, d), w1, w3, w2)


def _norm_proj_body(*refs, groups, n_extra, lowrank_fn):
    x_ref, g_ref, w_ref = refs[:3]
    n_in = 3 + (lowrank_fn is not None)
    extra = refs[n_in:n_in + n_extra]
    outs = refs[n_in + n_extra:n_in + n_extra + len(groups)]
    if lowrank_fn is None:
        h_scr = refs[-1]
    else:
        w_lr_ref, lr_out_ref, h_scr, lr_scr = refs[3], refs[-3], refs[-2], refs[-1]
    j = pl.program_id(1)

    def tile(h, fn, o_ref):
        acc = jnp.dot(h, w_ref[...], preferred_element_type=F32)
        o_ref[...] = fn(acc, extra).astype(o_ref.dtype)

    @pl.when(j == 0)
    def _():
        h = _rms(x_ref[...], g_ref[...]).astype(BF16)
        h_scr[...] = h
        if lowrank_fn is not None:
            lr_scr[...] = jnp.dot(h, w_lr_ref[...], preferred_element_type=F32).astype(BF16)
        tile(h, groups[0][3], outs[0])

    start = 0
    for (n_tiles, _, _, fn), o_ref in zip(groups, outs):
        @pl.when((j >= max(start, 1)) & (j < start + n_tiles))
        def _(fn=fn, o_ref=o_ref):
            tile(h_scr[...], fn, o_ref)
        start += n_tiles

    if lowrank_fn is not None:
        @pl.when(j == start)
        def _():
            lr_out_ref[...] = lowrank_fn(lr_scr[...], extra).astype(lr_out_ref.dtype)


def _norm_proj(x, gain, w, groups, extras, *, tm, tn, name, row_extras=(), lowrank=None):
    n, d = x.shape
    n_w_tiles = sum(g[0] for g in groups)
    assert w.shape[1] == tn * n_w_tiles
    in_specs = [
        pl.BlockSpec((tm, d), lambda i, j: (i, 0)),
        pl.BlockSpec((1, d), lambda i, j: (0, 0)),
        pl.BlockSpec((d, tn), lambda i, j: (0, jnp.minimum(j, n_w_tiles - 1))),
    ]
    operands = [x, gain.reshape(1, d), w]
    scratch = [pltpu.VMEM((tm, d), BF16)]
    if lowrank is not None:
        in_specs.append(pl.BlockSpec(lowrank[0].shape, lambda i, j: (0, 0)))
        operands.append(lowrank[0])
        scratch.append(pltpu.VMEM((tm, lowrank[0].shape[1]), BF16))
    in_specs += [pl.BlockSpec(e.shape, lambda i, j: (0, 0)) for e in extras] + [
        pl.BlockSpec((rows, a.shape[1]), lambda i, j, per=per: (i // per, 0))
        for a, rows, per in row_extras]
    operands += list(extras) + [a for a, _, _ in row_extras]
    out_specs, out_shapes = [], []
    start = 0
    for n_tiles, width, dtype, _ in groups:
        out_specs.append(pl.BlockSpec(
            (tm, width),
            lambda i, j, s=start, c=n_tiles: (i, jnp.clip(j - s, 0, c - 1))))
        out_shapes.append(jax.ShapeDtypeStruct((n, width * n_tiles), dtype))
        start += n_tiles
    if lowrank is not None:
        out_specs.append(pl.BlockSpec((tm, lowrank[1]), lambda i, j: (i, 0)))
        out_shapes.append(jax.ShapeDtypeStruct((n, lowrank[1]), lowrank[2]))
        start += 1
    return pl.pallas_call(
        functools.partial(_norm_proj_body, groups=groups,
                          n_extra=len(extras) + len(row_extras),
                          lowrank_fn=None if lowrank is None else lowrank[3]),
        grid=(n // tm, start),
        in_specs=in_specs,
        out_specs=out_specs,
        out_shape=out_shapes,
        scratch_shapes=scratch,
        compiler_params=_params("parallel", "arbitrary"),
        name=name,
    )(*operands)


def _scale_ep(scale):
    return lambda acc, extra: acc * scale


def _plain_ep(acc, extra):
    return acc


def _head_norm_ep(gain_idx, scale):
    def fn(acc, extra):
        gain = extra[gain_idx][...]
        parts = []
        for h in range(acc.shape[1] // HEAD_DIM):
            a = acc[:, h * HEAD_DIM:(h + 1) * HEAD_DIM]
            parts.append(_rms(a, gain) * scale)
        return jnp.concatenate(parts, axis=-1)
    return fn


def _mem_attn_ep(gain_idx, mk_idx, mv_idx):
    norm = _head_norm_ep(gain_idx, HEAD_DIM ** -0.5)

    def fn(acc, extra):
        qm = norm(acc, extra).astype(BF16)
        mk_ref, mv_ref = extra[mk_idx], extra[mv_idx]
        parts = []
        for h in range(MEM_HEADS):
            hs = slice(h * HEAD_DIM, (h + 1) * HEAD_DIM)
            s = lax.dot_general(qm[:, hs], mk_ref[:, hs], (((1,), (1,)), ((), ())),
                                preferred_element_type=F32)
            e = jnp.exp(s - jnp.max(s, axis=1, keepdims=True))
            p = e / jnp.sum(e, axis=1, keepdims=True)
            parts.append(jnp.dot(p.astype(BF16), mv_ref[:, hs], preferred_element_type=F32))
        return jnp.concatenate(parts, axis=-1)
    return fn


def _gla_gate_ep(w_up_idx, bias_idx):
    def fn(lr, extra):
        z = jnp.dot(lr, extra[w_up_idx][...], preferred_element_type=F32)
        return _log_sigmoid(z + extra[bias_idx][...]) / GLA_GATE_NORMALIZER
    return fn


def _forget_gate_ep(bias_idx):
    def fn(acc, extra):
        return _log_sigmoid(acc[:, :LANES] + extra[bias_idx][...])
    return fn


def _cumsum_body(lf_ref, col_ref, kd_ref, carry):
    @pl.when(pl.program_id(1) == 0)
    def _():
        carry[...] = jnp.zeros_like(carry)

    cum = _cumsum_rows(lf_ref[...]) + carry[...]
    carry[...] = cum[cum.shape[0] - 1:, :]
    c2 = cum * LOG2E
    col_ref[...] = c2
    terms = [term.astype(F32) for term in _split3(-c2)]
    lane = lax.broadcasted_iota(jnp.int32, c2.shape, 1)
    for h in range(kd_ref.shape[0]):
        tile = jnp.zeros_like(c2)
        for i, term in enumerate(terms):
            tile = jnp.where(lane == i, term[:, h:h + 1], tile)
        kd_ref[h] = tile.astype(BF16)


def _seq_cumsum(logf, bsz, seq, heads, *, ts):
    return pl.pallas_call(
        _cumsum_body,
        grid=(bsz, seq // ts),
        in_specs=[pl.BlockSpec((None, ts, LANES), lambda b, s: (b, s, 0))],
        out_specs=[pl.BlockSpec((None, ts, LANES), lambda b, s: (b, s, 0)),
                   pl.BlockSpec((None, heads, ts, LANES), lambda b, s: (b, 0, s, 0))],
        out_shape=[jax.ShapeDtypeStruct((bsz, seq, LANES), F32),
                   jax.ShapeDtypeStruct((bsz, heads, seq, LANES), BF16)],
        scratch_shapes=[pltpu.VMEM((1, LANES), F32)],
        compiler_params=_params("parallel", "arbitrary"),
        name="forget_cumsum",
    )(logf.reshape(bsz, seq, LANES))


def _gla_body(q_ref, k_ref, la_ref, v_ref, g_ref, gain_ref, o_ref, st_ref, *, n_chunk, hv):
    @pl.when(pl.program_id(1) == 0)
    def _():
        st_ref[...] = jnp.zeros_like(st_ref)

    gain = gain_ref[...]
    for c in range(n_chunk):
        rows = pl.ds(c * CHUNK, CHUNK)
        cum = _cumsum_rows(la_ref[rows, :])
        total = cum[CHUNK - 1:, :]
        k_dec = (k_ref[rows, :] * jnp.exp(total - cum)).astype(BF16)
        a_chunk = jnp.exp(total)
        for h in range(GLA_HEADS):
            ks = slice(h * GLA_HK_PAD, (h + 1) * GLA_HK_PAD)
            vs = slice(h * hv, (h + 1) * hv)
            upd = lax.dot_general(v_ref[rows, vs], k_dec[:, ks],
                                  (((0,), (0,)), ((), ())), preferred_element_type=F32)
            st = a_chunk[:, ks] * st_ref[h] + upd
            st_ref[h] = st
            o = lax.dot_general(q_ref[rows, ks], st.astype(BF16),
                                (((1,), (1,)), ((), ())), preferred_element_type=F32)
            o = _rms(o, gain) * _silu(g_ref[rows, vs].astype(F32))
            o_ref[rows, vs] = o.astype(BF16)


def _gla(q, k, la, v, g, gain, bsz, seq, *, n_chunk):
    hv = v.shape[1] // GLA_HEADS
    kw = q.shape[1]
    tc = n_chunk * CHUNK
    nt = seq // tc

    def spec(width):
        return pl.BlockSpec((tc, width), lambda b, c: (b * nt + c, 0))

    return pl.pallas_call(
        functools.partial(_gla_body, n_chunk=n_chunk, hv=hv),
        grid=(bsz, nt),
        in_specs=[spec(kw), spec(kw), spec(kw), spec(v.shape[1]), spec(v.shape[1]),
                  pl.BlockSpec((1, hv), lambda b, c: (0, 0))],
        out_specs=spec(v.shape[1]),
        out_shape=jax.ShapeDtypeStruct(v.shape, BF16),
        scratch_shapes=[pltpu.VMEM((GLA_HEADS, hv, GLA_HK_PAD), F32)],
        compiler_params=_params("parallel", "arbitrary"),
        name="gla",
    )(q, k, la, v, g, gain.reshape(1, hv))


def _fox_body(qi_tab, ki_tab, q_ref, k_ref, v_ref, vprev_ref, kd_ref, ccol_ref, g_ref, o_ref,
              qa_scr, ka_scr, p_scr, m_scr, l_scr, acc_scr, cc_scr, *, t, rq, hp):
    step = pl.program_id(2)
    head0 = pl.program_id(1) * hp
    qi = qi_tab[step]
    ki = ki_tab[step]

    def start_query_tile():
        m_scr[...] = jnp.full_like(m_scr, -jnp.inf)
        l_scr[...] = jnp.zeros_like(l_scr)
        acc_scr[...] = jnp.zeros_like(acc_scr)
        blk = ccol_ref[...]
        lane = lax.broadcasted_iota(jnp.int32, blk.shape, 1)
        for j in range(hp):
            cc_scr[j] = jnp.sum(jnp.where(lane == head0 + j, blk, 0.0), axis=1, keepdims=True)
            qa_scr[j, :, :HEAD_DIM] = q_ref[:, j * HEAD_DIM:(j + 1) * HEAD_DIM]
            qa_scr[j, :, HEAD_DIM:] = (lane < DECAY_TERMS).astype(BF16)

    def load_keys():
        for j in range(hp):
            ka_scr[j, :, :HEAD_DIM] = k_ref[:, j * HEAD_DIM:(j + 1) * HEAD_DIM]
            ka_scr[j, :, HEAD_DIM:] = kd_ref[j]

    def update(j, r, n_cols, masked, pending):
        rows = pl.ds(r * rq, rq)
        u = lax.dot_general(qa_scr[j, rows, :], ka_scr[j, :n_cols, :], (((1,), (1,)), ((), ())),
                            preferred_element_type=F32)
        if masked:
            tail = u[:, n_cols - rq:]
            row = lax.broadcasted_iota(jnp.int32, tail.shape, 0)
            col = lax.broadcasted_iota(jnp.int32, tail.shape, 1)
            tail = jnp.where(col <= row, tail, -jnp.inf)
            u = tail if n_cols == rq else jnp.concatenate([u[:, :n_cols - rq], tail], axis=1)
        cc = cc_scr[j, rows, :]
        m_prev = m_scr[j, rows, :]
        m_new = jnp.maximum(m_prev, jnp.max(u, axis=1, keepdims=True) + cc)
        p = jnp.exp2(u + (cc - m_new))
        alpha = jnp.exp2(m_prev - m_new)
        l_scr[j, rows, :] = alpha * l_scr[j, rows, :] + jnp.sum(p, axis=1, keepdims=True)
        m_scr[j, rows, :] = m_new
        acc = acc_scr[j, rows, :]
        if pending:
            acc = acc + jnp.dot(p_scr[j, rows, :], vprev_ref[:, j * HEAD_DIM:(j + 1) * HEAD_DIM],
                                preferred_element_type=F32)
        acc_scr[j, rows, :] = alpha * acc
        p_scr[j, rows, :n_cols] = p.astype(BF16)

    n_sub = t // rq

    def tile(masked, pending):
        for r in range(n_sub):
            for j in range(hp):
                update(j, r, (r + 1) * rq if masked else t, masked, pending)

    for diag in (False, True):
        for pending in (False, True):
            first = ki == 0

            @pl.when(((ki == qi) if diag else (ki < qi)) & (~first if pending else first))
            def _(diag=diag, pending=pending):
                if not pending:
                    start_query_tile()
                load_keys()
                tile(diag, pending)

    @pl.when(ki == qi)
    def _():
        for j in range(hp):
            hs = slice(j * HEAD_DIM, (j + 1) * HEAD_DIM)
            for r in range(n_sub):
                rows = pl.ds(r * rq, rq)
                n_cols = (r + 1) * rq
                acc_scr[j, rows, :] += jnp.dot(p_scr[j, rows, :n_cols], v_ref[:n_cols, hs],
                                               preferred_element_type=F32)
            o = acc_scr[j] / l_scr[j]
            o_ref[:, hs] = (o * jax.nn.sigmoid(g_ref[:, hs].astype(F32))).astype(BF16)


def _fox(q, k, v, kd, c2, g, *, t, rq, hp):
    bsz, seq, width = q.shape
    heads = width // HEAD_DIM
    hw = hp * HEAD_DIM
    pairs = [(qi, ki) for qi in range(seq // t) for ki in range(qi + 1)]
    qi_tab = jnp.asarray(np.array([p[0] for p in pairs], np.int32))
    ki_tab = jnp.asarray(np.array([p[1] for p in pairs], np.int32))
    q_spec = pl.BlockSpec((None, t, hw), lambda b, h, s, qt, kt: (b, qt[s], h))
    kv_spec = pl.BlockSpec((None, t, hw), lambda b, h, s, qt, kt: (b, kt[s], h))
    vprev_spec = pl.BlockSpec((None, t, hw),
                              lambda b, h, s, qt, kt: (b, jnp.maximum(kt[s] - 1, 0), h))
    grid_spec = pltpu.PrefetchScalarGridSpec(
        num_scalar_prefetch=2,
        grid=(bsz, heads // hp, len(pairs)),
        in_specs=[
            q_spec, kv_spec, kv_spec, vprev_spec,
            pl.BlockSpec((None, hp, t, LANES), lambda b, h, s, qt, kt: (b, h, kt[s], 0)),
            pl.BlockSpec((None, t, LANES), lambda b, h, s, qt, kt: (b, qt[s], 0)),
            q_spec,
        ],
        out_specs=q_spec,
        scratch_shapes=[pltpu.VMEM((hp, t, 2 * HEAD_DIM), BF16),
                        pltpu.VMEM((hp, t, 2 * HEAD_DIM), BF16),
                        pltpu.VMEM((hp, t, t), BF16),
                        pltpu.VMEM((hp, t, 1), F32), pltpu.VMEM((hp, t, 1), F32),
                        pltpu.VMEM((hp, t, HEAD_DIM), F32), pltpu.VMEM((hp, t, 1), F32)],
    )
    return pl.pallas_call(
        functools.partial(_fox_body, t=t, rq=rq, hp=hp),
        grid_spec=grid_spec,
        out_shape=jax.ShapeDtypeStruct(q.shape, BF16),
        compiler_params=_params("parallel", "parallel", "arbitrary"),
        name="fox",
    )(qi_tab, ki_tab, q, k, v, v, kd, c2, g)


def _out_proj_body(x_ref, o_ref, mo_ref, wo_ref, wm_ref, y_ref):
    y = x_ref[...] + jnp.dot(o_ref[...], wo_ref[...], preferred_element_type=F32)
    y_ref[...] = y + jnp.dot(mo_ref[...], wm_ref[...], preferred_element_type=F32)


def _out_proj(x, o, mo, w_o, w_m, *, tm):
    n, d = x.shape
    return pl.pallas_call(
        _out_proj_body,
        grid=(n // tm,),
        in_specs=[pl.BlockSpec((tm, d), lambda i: (i, 0)),
                  pl.BlockSpec((tm, o.shape[1]), lambda i: (i, 0)),
                  pl.BlockSpec((tm, mo.shape[1]), lambda i: (i, 0)),
                  pl.BlockSpec(w_o.shape, lambda i: (0, 0)),
                  pl.BlockSpec(w_m.shape, lambda i: (0, 0))],
        out_specs=pl.BlockSpec((tm, d), lambda i: (i, 0)),
        out_shape=jax.ShapeDtypeStruct((n, d), F32),
        compiler_params=_params("parallel"),
        name="out_proj",
    )(x, o, mo, w_o, w_m)


def _pad_cols(w, width):
    return jnp.pad(w, ((0, 0), (0, width - w.shape[1])))


def _pad_gla_heads(w, hk):
    lead = w.shape[:-1]
    w = w.reshape(lead + (GLA_HEADS, hk))
    w = jnp.pad(w, [(0, 0)] * len(lead) + [(0, 0), (0, GLA_HK_PAD - hk)])
    return w.reshape(lead + (GLA_HEADS * GLA_HK_PAD,))


def _forward(x, mem, ffn_norm, ffn_w1, ffn_w3, ffn_w2, mix_norm, mem_norm, w_mem_kv,
             mem_q_norm, mem_k_norm, w_out, a_w_in, a_w_gate_up, a_b_gate, a_out_norm,
             b_w_in, b_q_norm, kv_norm, w_kv, b_f, k_norm, *, tiles):
    bsz, seq, d = x.shape
    n = bsz * seq
    depth = ffn_norm.shape[0]
    n_a = a_w_in.shape[0]
    main_w = d - MEM_W
    gla_key = main_w // 2
    hk = gla_key // GLA_HEADS
    fox_heads = main_w // HEAD_DIM
    tn = tiles["proj_tn"]
    attn_scale = HEAD_DIM ** -0.5

    w1 = ffn_w1.astype(BF16)
    w3 = ffn_w3.astype(BF16)
    w2 = ffn_w2.astype(BF16)
    x2 = x.reshape(n, d)
    mem2 = mem.reshape(bsz * mem.shape[1], d)

    def ffn(x2, l, s):
        return _ffn(x2, ffn_norm[l, s], w1, w3, w2, l, s,
                    tm=tiles["ffn_tm"], tf=tiles["ffn_tf"])

    k_sh = v_sh = c2 = k_decay = None
    for l in range(depth):
        if l == n_a:
            w = jnp.concatenate([w_kv[:, :2 * main_w], _pad_cols(w_kv[:, 2 * main_w:], tn)],
                                axis=1).astype(BF16)
            bias = _pad_cols(b_f.reshape(1, fox_heads), LANES)
            k_sh, v_sh, logf = _norm_proj(
                x2, kv_norm, w,
                [(main_w // tn, tn, BF16, _head_norm_ep(0, 1.0)),
                 (main_w // tn, tn, BF16, _plain_ep),
                 (1, LANES, F32, _forget_gate_ep(1))],
                [k_norm.reshape(1, HEAD_DIM), bias],
                tm=tiles["proj_tm"], tn=tn, name="kv_proj")
            c2, k_decay = _seq_cumsum(logf, bsz, seq, fox_heads, ts=tiles["cum_ts"])

        x2 = ffn(x2, l, 0)

        mk, mv = _norm_proj(
            mem2, mem_norm[l], w_mem_kv[l].astype(BF16),
            [(MEM_W // tn, tn, BF16, _head_norm_ep(0, 1.0)),
             (MEM_W // tn, tn, BF16, _plain_ep)],
            [mem_k_norm[l].reshape(1, HEAD_DIM)],
            tm=tiles["mem_tm"], tn=tn, name="mem_kv_proj")
        assert MEM_W == tn, "the memory-attention epilogue needs all memory heads in one tile"
        mem_rows = [(mk, mem.shape[1], seq // tiles["proj_tm"]),
                    (mv, mem.shape[1], seq // tiles["proj_tm"])]

        if l < n_a:
            wi = a_w_in[l]
            c = np.cumsum([0, gla_key, gla_key, main_w, GLA_GATE_RANK, main_w, MEM_W])
            w = jnp.concatenate([
                _pad_gla_heads(wi[:, c[0]:c[1]], hk),
                _pad_gla_heads(wi[:, c[1]:c[2]], hk),
                wi[:, c[2]:c[3]],
                wi[:, c[4]:c[5]],
                wi[:, c[5]:c[6]],
            ], axis=1).astype(BF16)
            kw = GLA_HEADS * GLA_HK_PAD
            w_lr = _pad_cols(wi[:, c[3]:c[4]], LANES).astype(BF16)
            w_up = jnp.pad(_pad_gla_heads(a_w_gate_up[l], hk),
                           ((0, LANES - GLA_GATE_RANK), (0, 0))).astype(BF16)
            b_gate = _pad_gla_heads(a_b_gate[l].reshape(1, gla_key), hk)
            q, k, v, g, mo, la = _norm_proj(
                x2, mix_norm[l], w,
                [(kw // tn, tn, BF16, _scale_ep(hk ** -0.5)),
                 (kw // tn, tn, F32, _plain_ep),
                 (main_w // tn, tn, BF16, _plain_ep),
                 (main_w // tn, tn, BF16, _plain_ep),
                 (1, tn, BF16, _mem_attn_ep(0, 3, 4))],
                [mem_q_norm[l].reshape(1, HEAD_DIM), w_up, b_gate],
                tm=tiles["proj_tm"], tn=tn, name="gla_in_proj", row_extras=mem_rows,
                lowrank=(w_lr, kw, F32, _gla_gate_ep(1, 2)))
            o = _gla(q, k, la, v, g, a_out_norm[l], bsz, seq, n_chunk=tiles["gla_chunks"])
        else:
            jl = l - n_a
            q, g, mo = _norm_proj(
                x2, mix_norm[l], b_w_in[jl].astype(BF16),
                [(main_w // tn, tn, BF16, _head_norm_ep(0, attn_scale * LOG2E)),
                 (main_w // tn, tn, BF16, _plain_ep),
                 (1, tn, BF16, _mem_attn_ep(1, 2, 3))],
                [b_q_norm[jl].reshape(1, HEAD_DIM), mem_q_norm[l].reshape(1, HEAD_DIM)],
                tm=tiles["proj_tm"], tn=tn, name="fox_in_proj", row_extras=mem_rows)
            o = _fox(q.reshape(bsz, seq, main_w), k_sh.reshape(bsz, seq, main_w),
                     v_sh.reshape(bsz, seq, main_w), k_decay, c2,
                     g.reshape(bsz, seq, main_w), t=tiles["fox_t"], rq=tiles["fox_rq"], hp=tiles["fox_hp"])
            o = o.reshape(n, main_w)

        wo = w_out[l].astype(BF16)
        x2 = _out_proj(x2, o, mo, wo[:main_w], wo[main_w:], tm=tiles["out_tm"])
        x2 = ffn(x2, l, 1)
    return x2.reshape(bsz, seq, d)


_TILES = dict(ffn_tm=1024, ffn_tf=512, proj_tm=1024, proj_tn=512, mem_tm=512, cum_ts=512,
              gla_chunks=8, fox_t=1024, fox_rq=256, fox_hp=2, out_tm=512)


def kernel(x, mem, ffn_norm, ffn_w1, ffn_w3, ffn_w2, mix_norm, mem_norm, w_mem_kv, mem_q_norm, mem_k_norm, w_out, a_w_in, a_w_gate_up, a_b_gate, a_out_norm, b_w_in, b_q_norm, kv_norm, w_kv, b_f, k_norm):
    return _forward(x, mem, ffn_norm, ffn_w1, ffn_w3, ffn_w2, mix_norm, mem_norm, w_mem_kv,
                    mem_q_norm, mem_k_norm, w_out, a_w_in, a_w_gate_up, a_b_gate, a_out_norm,
                    b_w_in, b_q_norm, kv_norm, w_kv, b_f, k_norm, tiles=_TILES)
```

```python
import functools

import numpy as np
import jax
import jax.numpy as jnp
from jax import lax
from jax.experimental import pallas as pl
from jax.experimental.pallas import tpu as pltpu

F32 = jnp.float32
BF16 = jnp.bfloat16

CHUNK = 64
HEAD_DIM = 128
MEM_HEADS = 4
MEM_W = MEM_HEADS * HEAD_DIM
GLA_HEADS = 4
GLA_GATE_RANK = 16
GLA_GATE_NORMALIZER = 16.0
EPS = 1e-6
LOG2E = 1.4426950408889634
DECAY_TERMS = 3

LANES = 128
V7X_VMEM_LIMIT_BYTES = 56 * 1024 * 1024

GLA_HK_PAD = 256


def _rms(x, gain):
    return x * lax.rsqrt(jnp.mean(x * x, axis=-1, keepdims=True) + EPS) * gain


def _log_sigmoid(x):
    return jnp.minimum(x, 0.0) - jnp.log1p(jnp.exp(-jnp.abs(x)))


def _silu(x):
    return x * jax.nn.sigmoid(x)


def _split3(x):
    hi = x.astype(BF16)
    r1 = x - hi.astype(F32)
    mid = r1.astype(BF16)
    lo = (r1 - mid.astype(F32)).astype(BF16)
    return hi, mid, lo


def _cumsum_rows(x):
    n = x.shape[0]
    row = lax.broadcasted_iota(jnp.int32, (n, n), 0)
    col = lax.broadcasted_iota(jnp.int32, (n, n), 1)
    tri = (row >= col).astype(BF16)
    return sum(jnp.dot(tri, term, preferred_element_type=F32) for term in _split3(x))


def _params(*semantics):
    return pltpu.CompilerParams(dimension_semantics=semantics,
                                vmem_limit_bytes=V7X_VMEM_LIMIT_BYTES)


def _ffn_body(x_ref, g_ref, w1_ref, w3_ref, w2_ref, o_ref, h_scr):
    def ff_tile(h):
        a = jnp.dot(h, w1_ref[...], preferred_element_type=F32)
        b = jnp.dot(h, w3_ref[...], preferred_element_type=F32)
        act = (_silu(a) * b * 0.5).astype(BF16)
        return jnp.dot(act, w2_ref[...], preferred_element_type=F32)

    @pl.when(pl.program_id(1) == 0)
    def _():
        x = x_ref[...]
        h = _rms(x, g_ref[...]).astype(BF16)
        h_scr[...] = h
        o_ref[...] = x + ff_tile(h)

    @pl.when(pl.program_id(1) > 0)
    def _():
        o_ref[...] += ff_tile(h_scr[...])


def _ffn(x, gain, w1, w3, w2, layer, half, *, tm, tf):
    n, d = x.shape
    f = w1.shape[-1]
    return pl.pallas_call(
        _ffn_body,
        grid=(n // tm, f // tf),
        in_specs=[
            pl.BlockSpec((tm, d), lambda i, j: (i, 0)),
            pl.BlockSpec((1, d), lambda i, j: (0, 0)),
            pl.BlockSpec((None, None, d, tf), lambda i, j: (layer, half, 0, j)),
            pl.BlockSpec((None, None, d, tf), lambda i, j: (layer, half, 0, j)),
            pl.BlockSpec((None, None, tf, d), lambda i, j: (layer, half, j, 0)),
        ],
        out_specs=pl.BlockSpec((tm, d), lambda i, j: (i, 0)),
        out_shape=jax.ShapeDtypeStruct((n, d), F32),
        scratch_shapes=[pltpu.VMEM((tm, d), BF16)],
        compiler_params=_params("parallel", "arbitrary"),
        name="ffn",
    )(x, gain.reshape(1, d), w1, w3, w2)


def _norm_proj_body(*refs, groups, n_extra, lowrank_fn):
    x_ref, g_ref, w_ref = refs[:3]
    n_in = 3 + (lowrank_fn is not None)
    extra = refs[n_in:n_in + n_extra]
    outs = refs[n_in + n_extra:n_in + n_extra + len(groups)]
    if lowrank_fn is None:
        h_scr = refs[-1]
    else:
        w_lr_ref, lr_out_ref, h_scr, lr_scr = refs[3], refs[-3], refs[-2], refs[-1]
    j = pl.program_id(1)

    def tile(h, fn, o_ref):
        acc = jnp.dot(h, w_ref[...], preferred_element_type=F32)
        o_ref[...] = fn(acc, extra).astype(o_ref.dtype)

    @pl.when(j == 0)
    def _():
        h = _rms(x_ref[...], g_ref[...]).astype(BF16)
        h_scr[...] = h
        if lowrank_fn is not None:
            lr_scr[...] = jnp.dot(h, w_lr_ref[...], preferred_element_type=F32).astype(BF16)
        tile(h, groups[0][3], outs[0])

    start = 0
    for (n_tiles, _, _, fn), o_ref in zip(groups, outs):
        @pl.when((j >= max(start, 1)) & (j < start + n_tiles))
        def _(fn=fn, o_ref=o_ref):
            tile(h_scr[...], fn, o_ref)
        start += n_tiles

    if lowrank_fn is not None:
        @pl.when(j == start)
        def _():
            lr_out_ref[...] = lowrank_fn(lr_scr[...], extra).astype(lr_out_ref.dtype)


def _norm_proj(x, gain, w, groups, extras, *, tm, tn, name, row_extras=(), lowrank=None):
    n, d = x.shape
    n_w_tiles = sum(g[0] for g in groups)
    assert w.shape[1] == tn * n_w_tiles
    in_specs = [
        pl.BlockSpec((tm, d), lambda i, j: (i, 0)),
        pl.BlockSpec((1, d), lambda i, j: (0, 0)),
        pl.BlockSpec((d, tn), lambda i, j: (0, jnp.minimum(j, n_w_tiles - 1))),
    ]
    operands = [x, gain.reshape(1, d), w]
    scratch = [pltpu.VMEM((tm, d), BF16)]
    if lowrank is not None:
        in_specs.append(pl.BlockSpec(lowrank[0].shape, lambda i, j: (0, 0)))
        operands.append(lowrank[0])
        scratch.append(pltpu.VMEM((tm, lowrank[0].shape[1]), BF16))
    in_specs += [pl.BlockSpec(e.shape, lambda i, j: (0, 0)) for e in extras] + [
        pl.BlockSpec((rows, a.shape[1]), lambda i, j, per=per: (i // per, 0))
        for a, rows, per in row_extras]
    operands += list(extras) + [a for a, _, _ in row_extras]
    out_specs, out_shapes = [], []
    start = 0
    for n_tiles, width, dtype, _ in groups:
        out_specs.append(pl.BlockSpec(
            (tm, width),
            lambda i, j, s=start, c=n_tiles: (i, jnp.clip(j - s, 0, c - 1))))
        out_shapes.append(jax.ShapeDtypeStruct((n, width * n_tiles), dtype))
        start += n_tiles
    if lowrank is not None:
        out_specs.append(pl.BlockSpec((tm, lowrank[1]), lambda i, j: (i, 0)))
        out_shapes.append(jax.ShapeDtypeStruct((n, lowrank[1]), lowrank[2]))
        start += 1
    return pl.pallas_call(
        functools.partial(_norm_proj_body, groups=groups,
                          n_extra=len(extras) + len(row_extras),
                          lowrank_fn=None if lowrank is None else lowrank[3]),
        grid=(n // tm, start),
        in_specs=in_specs,
        out_specs=out_specs,
        out_shape=out_shapes,
        scratch_shapes=scratch,
        compiler_params=_params("parallel", "arbitrary"),
        name=name,
    )(*operands)


def _scale_ep(scale):
    return lambda acc, extra: acc * scale


def _plain_ep(acc, extra):
    return acc


def _head_norm_ep(gain_idx, scale):
    def fn(acc, extra):
        gain = extra[gain_idx][...]
        parts = []
        for h in range(acc.shape[1] // HEAD_DIM):
            a = acc[:, h * HEAD_DIM:(h + 1) * HEAD_DIM]
            parts.append(_rms(a, gain) * scale)
        return jnp.concatenate(parts, axis=-1)
    return fn


def _mem_attn_ep(gain_idx, mk_idx, mv_idx):
    norm = _head_norm_ep(gain_idx, HEAD_DIM ** -0.5)

    def fn(acc, extra):
        qm = norm(acc, extra).astype(BF16)
        mk_ref, mv_ref = extra[mk_idx], extra[mv_idx]
        parts = []
        for h in range(MEM_HEADS):
            hs = slice(h * HEAD_DIM, (h + 1) * HEAD_DIM)
            s = lax.dot_general(qm[:, hs], mk_ref[:, hs], (((1,), (1,)), ((), ())),
                                preferred_element_type=F32)
            e = jnp.exp(s - jnp.max(s, axis=1, keepdims=True))
            p = e / jnp.sum(e, axis=1, keepdims=True)
            parts.append(jnp.dot(p.astype(BF16), mv_ref[:, hs], preferred_element_type=F32))
        return jnp.concatenate(parts, axis=-1)
    return fn


def _gla_gate_ep(w_up_idx, bias_idx):
    def fn(lr, extra):
        z = jnp.dot(lr, extra[w_up_idx][...], preferred_element_type=F32)
        return _log_sigmoid(z + extra[bias_idx][...]) / GLA_GATE_NORMALIZER
    return fn


def _forget_gate_ep(bias_idx):
    def fn(acc, extra):
        return _log_sigmoid(acc[:, :LANES] + extra[bias_idx][...])
    return fn


def _cumsum_body(lf_ref, col_ref, kd_ref, carry):
    @pl.when(pl.program_id(1) == 0)
    def _():
        carry[...] = jnp.zeros_like(carry)

    cum = _cumsum_rows(lf_ref[...]) + carry[...]
    carry[...] = cum[cum.shape[0] - 1:, :]
    c2 = cum * LOG2E
    col_ref[...] = c2
    terms = [term.astype(F32) for term in _split3(-c2)]
    lane = lax.broadcasted_iota(jnp.int32, c2.shape, 1)
    for h in range(kd_ref.shape[0]):
        tile = jnp.zeros_like(c2)
        for i, term in enumerate(terms):
            tile = jnp.where(lane == i, term[:, h:h + 1], tile)
        kd_ref[h] = tile.astype(BF16)


def _seq_cumsum(logf, bsz, seq, heads, *, ts):
    return pl.pallas_call(
        _cumsum_body,
        grid=(bsz, seq // ts),
        in_specs=[pl.BlockSpec((None, ts, LANES), lambda b, s: (b, s, 0))],
        out_specs=[pl.BlockSpec((None, ts, LANES), lambda b, s: (b, s, 0)),
                   pl.BlockSpec((None, heads, ts, LANES), lambda b, s: (b, 0, s, 0))],
        out_shape=[jax.ShapeDtypeStruct((bsz, seq, LANES), F32),
                   jax.ShapeDtypeStruct((bsz, heads, seq, LANES), BF16)],
        scratch_shapes=[pltpu.VMEM((1, LANES), F32)],
        compiler_params=_params("parallel", "arbitrary"),
        name="forget_cumsum",
    )(logf.reshape(bsz, seq, LANES))


def _gla_body(q_ref, k_ref, la_ref, v_ref, g_ref, gain_ref, o_ref, st_ref, *, n_chunk, hv):
    @pl.when(pl.program_id(1) == 0)
    def _():
        st_ref[...] = jnp.zeros_like(st_ref)

    gain = gain_ref[...]
    for c in range(n_chunk):
        rows = pl.ds(c * CHUNK, CHUNK)
        cum = _cumsum_rows(la_ref[rows, :])
        total = cum[CHUNK - 1:, :]
        k_dec = (k_ref[rows, :] * jnp.exp(total - cum)).astype(BF16)
        a_chunk = jnp.exp(total)
        for h in range(GLA_HEADS):
            ks = slice(h * GLA_HK_PAD, (h + 1) * GLA_HK_PAD)
            vs = slice(h * hv, (h + 1) * hv)
            upd = lax.dot_general(v_ref[rows, vs], k_dec[:, ks],
                                  (((0,), (0,)), ((), ())), preferred_element_type=F32)
            st = a_chunk[:, ks] * st_ref[h] + upd
            st_ref[h] = st
            o = lax.dot_general(q_ref[rows, ks], st.astype(BF16),
                                (((1,), (1,)), ((), ())), preferred_element_type=F32)
            o = _rms(o, gain) * _silu(g_ref[rows, vs].astype(F32))
            o_ref[rows, vs] = o.astype(BF16)


def _gla(q, k, la, v, g, gain, bsz, seq, *, n_chunk):
    hv = v.shape[1] // GLA_HEADS
    kw = q.shape[1]
    tc = n_chunk * CHUNK
    nt = seq // tc

    def spec(width):
        return pl.BlockSpec((tc, width), lambda b, c: (b * nt + c, 0))

    return pl.pallas_call(
        functools.partial(_gla_body, n_chunk=n_chunk, hv=hv),
        grid=(bsz, nt),
        in_specs=[spec(kw), spec(kw), spec(kw), spec(v.shape[1]), spec(v.shape[1]),
                  pl.BlockSpec((1, hv), lambda b, c: (0, 0))],
        out_specs=spec(v.shape[1]),
        out_shape=jax.ShapeDtypeStruct(v.shape, BF16),
        scratch_shapes=[pltpu.VMEM((GLA_HEADS, hv, GLA_HK_PAD), F32)],
        compiler_params=_params("parallel", "arbitrary"),
        name="gla",
    )(q, k, la, v, g, gain.reshape(1, hv))


def _fox_body(qi_tab, ki_tab, q_ref, k_ref, v_ref, vprev_ref, kd_ref, ccol_ref, g_ref, o_ref,
              qa_scr, ka_scr, p_scr, m_scr, l_scr, acc_scr, cc_scr, *, t, rq, hp):
    step = pl.program_id(2)
    head0 = pl.program_id(1) * hp
    qi = qi_tab[step]
    ki = ki_tab[step]

    def start_query_tile():
        m_scr[...] = jnp.full_like(m_scr, -jnp.inf)
        l_scr[...] = jnp.zeros_like(l_scr)
        acc_scr[...] = jnp.zeros_like(acc_scr)
        blk = ccol_ref[...]
        lane = lax.broadcasted_iota(jnp.int32, blk.shape, 1)
        for j in range(hp):
            cc_scr[j] = jnp.sum(jnp.where(lane == head0 + j, blk, 0.0), axis=1, keepdims=True)
            qa_scr[j, :, :HEAD_DIM] = q_ref[:, j * HEAD_DIM:(j + 1) * HEAD_DIM]
            qa_scr[j, :, HEAD_DIM:] = (lane < DECAY_TERMS).astype(BF16)

    def load_keys():
        for j in range(hp):
            ka_scr[j, :, :HEAD_DIM] = k_ref[:, j * HEAD_DIM:(j + 1) * HEAD_DIM]
            ka_scr[j, :, HEAD_DIM:] = kd_ref[j]

    def update(j, r, n_cols, masked, pending):
        rows = pl.ds(r * rq, rq)
        u = lax.dot_general(qa_scr[j, rows, :], ka_scr[j, :n_cols, :], (((1,), (1,)), ((), ())),
                            preferred_element_type=F32)
        if masked:
            tail = u[:, n_cols - rq:]
            row = lax.broadcasted_iota(jnp.int32, tail.shape, 0)
            col = lax.broadcasted_iota(jnp.int32, tail.shape, 1)
            tail = jnp.where(col <= row, tail, -jnp.inf)
            u = tail if n_cols == rq else jnp.concatenate([u[:, :n_cols - rq], tail], axis=1)
        cc = cc_scr[j, rows, :]
        m_prev = m_scr[j, rows, :]
        m_new = jnp.maximum(m_prev, jnp.max(u, axis=1, keepdims=True) + cc)
        p = jnp.exp2(u + (cc - m_new))
        alpha = jnp.exp2(m_prev - m_new)
        l_scr[j, rows, :] = alpha * l_scr[j, rows, :] + jnp.sum(p, axis=1, keepdims=True)
        m_scr[j, rows, :] = m_new
        acc = acc_scr[j, rows, :]
        if pending:
            acc = acc + jnp.dot(p_scr[j, rows, :], vprev_ref[:, j * HEAD_DIM:(j + 1) * HEAD_DIM],
                                preferred_element_type=F32)
        acc_scr[j, rows, :] = alpha * acc
        p_scr[j, rows, :n_cols] = p.astype(BF16)

    n_sub = t // rq

    def tile(masked, pending):
        for r in range(n_sub):
            for j in range(hp):
                update(j, r, (r + 1) * rq if masked else t, masked, pending)

    for diag in (False, True):
        for pending in (False, True):
            first = ki == 0

            @pl.when(((ki == qi) if diag else (ki < qi)) & (~first if pending else first))
            def _(diag=diag, pending=pending):
                if not pending:
                    start_query_tile()
                load_keys()
                tile(diag, pending)

    @pl.when(ki == qi)
    def _():
        for j in range(hp):
            hs = slice(j * HEAD_DIM, (j + 1) * HEAD_DIM)
            for r in range(n_sub):
                rows = pl.ds(r * rq, rq)
                n_cols = (r + 1) * rq
                acc_scr[j, rows, :] += jnp.dot(p_scr[j, rows, :n_cols], v_ref[:n_cols, hs],
                                               preferred_element_type=F32)
            o = acc_scr[j] / l_scr[j]
            o_ref[:, hs] = (o * jax.nn.sigmoid(g_ref[:, hs].astype(F32))).astype(BF16)


def _fox(q, k, v, kd, c2, g, *, t, rq, hp):
    bsz, seq, width = q.shape
    heads = width // HEAD_DIM
    hw = hp * HEAD_DIM
    pairs = [(qi, ki) for qi in range(seq // t) for ki in range(qi + 1)]
    qi_tab = jnp.asarray(np.array([p[0] for p in pairs], np.int32))
    ki_tab = jnp.asarray(np.array([p[1] for p in pairs], np.int32))
    q_spec = pl.BlockSpec((None, t, hw), lambda b, h, s, qt, kt: (b, qt[s], h))
    kv_spec = pl.BlockSpec((None, t, hw), lambda b, h, s, qt, kt: (b, kt[s], h))
    vprev_spec = pl.BlockSpec((None, t, hw),
                              lambda b, h, s, qt, kt: (b, jnp.maximum(kt[s] - 1, 0), h))
    grid_spec = pltpu.PrefetchScalarGridSpec(
        num_scalar_prefetch=2,
        grid=(bsz, heads // hp, len(pairs)),
        in_specs=[
            q_spec, kv_spec, kv_spec, vprev_spec,
            pl.BlockSpec((None, hp, t, LANES), lambda b, h, s, qt, kt: (b, h, kt[s], 0)),
            pl.BlockSpec((None, t, LANES), lambda b, h, s, qt, kt: (b, qt[s], 0)),
            q_spec,
        ],
        out_specs=q_spec,
        scratch_shapes=[pltpu.VMEM((hp, t, 2 * HEAD_DIM), BF16),
                        pltpu.VMEM((hp, t, 2 * HEAD_DIM), BF16),
                        pltpu.VMEM((hp, t, t), BF16),
                        pltpu.VMEM((hp, t, 1), F32), pltpu.VMEM((hp, t, 1), F32),
                        pltpu.VMEM((hp, t, HEAD_DIM), F32), pltpu.VMEM((hp, t, 1), F32)],
    )
    return pl.pallas_call(
        functools.partial(_fox_body, t=t, rq=rq, hp=hp),
        grid_spec=grid_spec,
        out_shape=jax.ShapeDtypeStruct(q.shape, BF16),
        compiler_params=_params("parallel", "parallel", "arbitrary"),
        name="fox",
    )(qi_tab, ki_tab, q, k, v, v, kd, c2, g)


def _out_proj_body(x_ref, o_ref, mo_ref, wo_ref, wm_ref, y_ref):
    y = x_ref[...] + jnp.dot(o_ref[...], wo_ref[...], preferred_element_type=F32)
    y_ref[...] = y + jnp.dot(mo_ref[...], wm_ref[...], preferred_element_type=F32)


def _out_proj(x, o, mo, w_o, w_m, *, tm):
    n, d = x.shape
    return pl.pallas_call(
        _out_proj_body,
        grid=(n // tm,),
        in_specs=[pl.BlockSpec((tm, d), lambda i: (i, 0)),
                  pl.BlockSpec((tm, o.shape[1]), lambda i: (i, 0)),
                  pl.BlockSpec((tm, mo.shape[1]), lambda i: (i, 0)),
                  pl.BlockSpec(w_o.shape, lambda i: (0, 0)),
                  pl.BlockSpec(w_m.shape, lambda i: (0, 0))],
        out_specs=pl.BlockSpec((tm, d), lambda i: (i, 0)),
        out_shape=jax.ShapeDtypeStruct((n, d), F32),
        compiler_params=_params("parallel"),
        name="out_proj",
    )(x, o, mo, w_o, w_m)


def _pad_cols(w, width):
    return jnp.pad(w, ((0, 0), (0, width - w.shape[1])))


def _pad_gla_heads(w, hk):
    lead = w.shape[:-1]
    w = w.reshape(lead + (GLA_HEADS, hk))
    w = jnp.pad(w, [(0, 0)] * len(lead) + [(0, 0), (0, GLA_HK_PAD - hk)])
    return w.reshape(lead + (GLA_HEADS * GLA_HK_PAD,))


def _forward(x, mem, ffn_norm, ffn_w1, ffn_w3, ffn_w2, mix_norm, mem_norm, w_mem_kv,
             mem_q_norm, mem_k_norm, w_out, a_w_in, a_w_gate_up, a_b_gate, a_out_norm,
             b_w_in, b_q_norm, kv_norm, w_kv, b_f, k_norm, *, tiles):
    bsz, seq, d = x.shape
    n = bsz * seq
    depth = ffn_norm.shape[0]
    n_a = a_w_in.shape[0]
    main_w = d - MEM_W
    gla_key = main_w // 2
    hk = gla_key // GLA_HEADS
    fox_heads = main_w // HEAD_DIM
    tn = tiles["proj_tn"]
    attn_scale = HEAD_DIM ** -0.5

    w1 = ffn_w1.astype(BF16)
    w3 = ffn_w3.astype(BF16)
    w2 = ffn_w2.astype(BF16)
    x2 = x.reshape(n, d)
    mem2 = mem.reshape(bsz * mem.shape[1], d)

    def ffn(x2, l, s):
        return _ffn(x2, ffn_norm[l, s], w1, w3, w2, l, s,
                    tm=tiles["ffn_tm"], tf=tiles["ffn_tf"])

    k_sh = v_sh = c2 = k_decay = None
    for l in range(depth):
        if l == n_a:
            w = jnp.concatenate([w_kv[:, :2 * main_w], _pad_cols(w_kv[:, 2 * main_w:], tn)],
                                axis=1).astype(BF16)
            bias = _pad_cols(b_f.reshape(1, fox_heads), LANES)
            k_sh, v_sh, logf = _norm_proj(
                x2, kv_norm, w,
                [(main_w // tn, tn, BF16, _head_norm_ep(0, 1.0)),
                 (main_w // tn, tn, BF16, _plain_ep),
                 (1, LANES, F32, _forget_gate_ep(1))],
                [k_norm.reshape(1, HEAD_DIM), bias],
                tm=tiles["proj_tm"], tn=tn, name="kv_proj")
            c2, k_decay = _seq_cumsum(logf, bsz, seq, fox_heads, ts=tiles["cum_ts"])

        x2 = ffn(x2, l, 0)

        mk, mv = _norm_proj(
            mem2, mem_norm[l], w_mem_kv[l].astype(BF16),
            [(MEM_W // tn, tn, BF16, _head_norm_ep(0, 1.0)),
             (MEM_W // tn, tn, BF16, _plain_ep)],
            [mem_k_norm[l].reshape(1, HEAD_DIM)],
            tm=tiles["mem_tm"], tn=tn, name="mem_kv_proj")
        assert MEM_W == tn, "the memory-attention epilogue needs all memory heads in one tile"
        mem_rows = [(mk, mem.shape[1], seq // tiles["proj_tm"]),
                    (mv, mem.shape[1], seq // tiles["proj_tm"])]

        if l < n_a:
            wi = a_w_in[l]
            c = np.cumsum([0, gla_key, gla_key, main_w, GLA_GATE_RANK, main_w, MEM_W])
            w = jnp.concatenate([
                _pad_gla_heads(wi[:, c[0]:c[1]], hk),
                _pad_gla_heads(wi[:, c[1]:c[2]], hk),
                wi[:, c[2]:c[3]],
                wi[:, c[4]:c[5]],
                wi[:, c[5]:c[6]],
            ], axis=1).astype(BF16)
            kw = GLA_HEADS * GLA_HK_PAD
            w_lr = _pad_cols(wi[:, c[3]:c[4]], LANES).astype(BF16)
            w_up = jnp.pad(_pad_gla_heads(a_w_gate_up[l], hk),
                           ((0, LANES - GLA_GATE_RANK), (0, 0))).astype(BF16)
            b_gate = _pad_gla_heads(a_b_gate[l].reshape(1, gla_key), hk)
            q, k, v, g, mo, la = _norm_proj(
                x2, mix_norm[l], w,
                [(kw // tn, tn, BF16, _scale_ep(hk ** -0.5)),
                 (kw // tn, tn, F32, _plain_ep),
                 (main_w // tn, tn, BF16, _plain_ep),
                 (main_w // tn, tn, BF16, _plain_ep),
                 (1, tn, BF16, _mem_attn_ep(0, 3, 4))],
                [mem_q_norm[l].reshape(1, HEAD_DIM), w_up, b_gate],
                tm=tiles["proj_tm"], tn=tn, name="gla_in_proj", row_extras=mem_rows,
                lowrank=(w_lr, kw, F32, _gla_gate_ep(1, 2)))
            o = _gla(q, k, la, v, g, a_out_norm[l], bsz, seq, n_chunk=tiles["gla_chunks"])
        else:
            jl = l - n_a
            q, g, mo = _norm_proj(
                x2, mix_norm[l], b_w_in[jl].astype(BF16),
                [(main_w // tn, tn, BF16, _head_norm_ep(0, attn_scale * LOG2E)),
                 (main_w // tn, tn, BF16, _plain_ep),
                 (1, tn, BF16, _mem_attn_ep(1, 2, 3))],
                [b_q_norm[jl].reshape(1, HEAD_DIM), mem_q_norm[l].reshape(1, HEAD_DIM)],
                tm=tiles["proj_tm"], tn=tn, name="fox_in_proj", row_extras=mem_rows)
            o = _fox(q.reshape(bsz, seq, main_w), k_sh.reshape(bsz, seq, main_w),
                     v_sh.reshape(bsz, seq, main_w), k_decay, c2,
                     g.reshape(bsz, seq, main_w), t=tiles["fox_t"], rq=tiles["fox_rq"], hp=tiles["fox_hp"])
            o = o.reshape(n, main_w)

        wo = w_out[l].astype(BF16)
        x2 = _out_proj(x2, o, mo, wo[:main_w], wo[main_w:], tm=tiles["out_tm"])
        x2 = ffn(x2, l, 1)
    return x2.reshape(bsz, seq, d)


_TILES = dict(ffn_tm=1024, ffn_tf=512, proj_tm=1024, proj_tn=512, mem_tm=512, cum_ts=512,
              gla_chunks=16, fox_t=1024, fox_rq=256, fox_hp=4, out_tm=512)


def kernel(x, mem, ffn_norm, ffn_w1, ffn_w3, ffn_w2, mix_norm, mem_norm, w_mem_kv, mem_q_norm, mem_k_norm, w_out, a_w_in, a_w_gate_up, a_b_gate, a_out_norm, b_w_in, b_q_norm, kv_norm, w_kv, b_f, k_norm):
    return _forward(x, mem, ffn_norm, ffn_w1, ffn_w3, ffn_w2, mix_norm, mem_norm, w_mem_kv,
                    mem_q_norm, mem_k_norm, w_out, a_w_in, a_w_gate_up, a_b_gate, a_out_norm,
                    b_w_in, b_q_norm, kv_norm, w_kv, b_f, k_norm, tiles=_TILES)
```

```python
import functools

import numpy as np
import jax
import jax.numpy as jnp
from jax import lax
from jax.experimental import pallas as pl
from jax.experimental.pallas import tpu as pltpu

F32 = jnp.float32
BF16 = jnp.bfloat16

CHUNK = 64
HEAD_DIM = 128
MEM_HEADS = 4
MEM_W = MEM_HEADS * HEAD_DIM
GLA_HEADS = 4
GLA_GATE_RANK = 16
GLA_GATE_NORMALIZER = 16.0
EPS = 1e-6
LOG2E = 1.4426950408889634
DECAY_TERMS = 3

LANES = 128
V7X_VMEM_LIMIT_BYTES = 56 * 1024 * 1024

GLA_HK_PAD = 256


def _rms(x, gain):
    return x * lax.rsqrt(jnp.mean(x * x, axis=-1, keepdims=True) + EPS) * gain


def _log_sigmoid(x):
    return jnp.minimum(x, 0.0) - jnp.log1p(jnp.exp(-jnp.abs(x)))


def _silu(x):
    return x * jax.nn.sigmoid(x)


def _split3(x):
    hi = x.astype(BF16)
    r1 = x - hi.astype(F32)
    mid = r1.astype(BF16)
    lo = (r1 - mid.astype(F32)).astype(BF16)
    return hi, mid, lo


def _cumsum_rows(x):
    n = x.shape[0]
    row = lax.broadcasted_iota(jnp.int32, (n, n), 0)
    col = lax.broadcasted_iota(jnp.int32, (n, n), 1)
    tri = (row >= col).astype(BF16)
    return sum(jnp.dot(tri, term, preferred_element_type=F32) for term in _split3(x))


def _params(*semantics):
    return pltpu.CompilerParams(dimension_semantics=semantics,
                                vmem_limit_bytes=V7X_VMEM_LIMIT_BYTES)


def _ffn_body(x_ref, g_ref, w1_ref, w3_ref, w2_ref, o_ref, h_scr):
    def ff_tile(h):
        a = jnp.dot(h, w1_ref[...], preferred_element_type=F32)
        b = jnp.dot(h, w3_ref[...], preferred_element_type=F32)
        act = (_silu(a) * b * 0.5).astype(BF16)
        return jnp.dot(act, w2_ref[...], preferred_element_type=F32)

    @pl.when(pl.program_id(1) == 0)
    def _():
        x = x_ref[...]
        h = _rms(x, g_ref[...]).astype(BF16)
        h_scr[...] = h
        o_ref[...] = x + ff_tile(h)

    @pl.when(pl.program_id(1) > 0)
    def _():
        o_ref[...] += ff_tile(h_scr[...])


def _ffn(x, gain, w1, w3, w2, layer, half, *, tm, tf):
    n, d = x.shape
    f = w1.shape[-1]
    return pl.pallas_call(
        _ffn_body,
        grid=(n // tm, f // tf),
        in_specs=[
            pl.BlockSpec((tm, d), lambda i, j: (i, 0)),
            pl.BlockSpec((1, d), lambda i, j: (0, 0)),
            pl.BlockSpec((None, None, d, tf), lambda i, j: (layer, half, 0, j)),
            pl.BlockSpec((None, None, d, tf), lambda i, j: (layer, half, 0, j)),
            pl.BlockSpec((None, None, tf, d), lambda i, j: (layer, half, j, 0)),
        ],
        out_specs=pl.BlockSpec((tm, d), lambda i, j: (i, 0)),
        out_shape=jax.ShapeDtypeStruct((n, d), F32),
        scratch_shapes=[pltpu.VMEM((tm, d), BF16)],
        compiler_params=_params("parallel", "arbitrary"),
        name="ffn",
    )(x, gain.reshape(1, d), w1, w3, w2)


def _norm_proj_body(*refs, groups, n_extra, lowrank_fn):
    x_ref, g_ref, w_ref = refs[:3]
    n_in = 3 + (lowrank_fn is not None)
    extra = refs[n_in:n_in + n_extra]
    outs = refs[n_in + n_extra:n_in + n_extra + len(groups)]
    if lowrank_fn is None:
        h_scr = refs[-1]
    else:
        w_lr_ref, lr_out_ref, h_scr, lr_scr = refs[3], refs[-3], refs[-2], refs[-1]
    j = pl.program_id(1)

    def tile(h, fn, o_ref):
        acc = jnp.dot(h, w_ref[...], preferred_element_type=F32)
        o_ref[...] = fn(acc, extra).astype(o_ref.dtype)

    @pl.when(j == 0)
    def _():
        h = _rms(x_ref[...], g_ref[...]).astype(BF16)
        h_scr[...] = h
        if lowrank_fn is not None:
            lr_scr[...] = jnp.dot(h, w_lr_ref[...], preferred_element_type=F32).astype(BF16)
        tile(h, groups[0][3], outs[0])

    start = 0
    for (n_tiles, _, _, fn), o_ref in zip(groups, outs):
        @pl.when((j >= max(start, 1)) & (j < start + n_tiles))
        def _(fn=fn, o_ref=o_ref):
            tile(h_scr[...], fn, o_ref)
        start += n_tiles

    if lowrank_fn is not None:
        @pl.when(j == start)
        def _():
            lr_out_ref[...] = lowrank_fn(lr_scr[...], extra).astype(lr_out_ref.dtype)


def _norm_proj(x, gain, w, layer, groups, extras, *, tm, tn, name, row_extras=(), lowrank=None):
    n, d = x.shape
    n_w_tiles = sum(g[0] for g in groups)
    assert w.shape[2] == tn * n_w_tiles
    in_specs = [
        pl.BlockSpec((tm, d), lambda i, j: (i, 0)),
        pl.BlockSpec((1, d), lambda i, j: (0, 0)),
        pl.BlockSpec((None, d, tn), lambda i, j: (layer, 0, jnp.minimum(j, n_w_tiles - 1))),
    ]
    operands = [x, gain.reshape(1, d), w]
    scratch = [pltpu.VMEM((tm, d), BF16)]
    if lowrank is not None:
        in_specs.append(pl.BlockSpec(lowrank[0].shape, lambda i, j: (0, 0)))
        operands.append(lowrank[0])
        scratch.append(pltpu.VMEM((tm, lowrank[0].shape[1]), BF16))
    in_specs += [pl.BlockSpec(e.shape, lambda i, j: (0, 0)) for e in extras] + [
        pl.BlockSpec((rows, a.shape[1]), lambda i, j, per=per: (i // per, 0))
        for a, rows, per in row_extras]
    operands += list(extras) + [a for a, _, _ in row_extras]
    out_specs, out_shapes = [], []
    start = 0
    for n_tiles, width, dtype, _ in groups:
        out_specs.append(pl.BlockSpec(
            (tm, width),
            lambda i, j, s=start, c=n_tiles: (i, jnp.clip(j - s, 0, c - 1))))
        out_shapes.append(jax.ShapeDtypeStruct((n, width * n_tiles), dtype))
        start += n_tiles
    if lowrank is not None:
        out_specs.append(pl.BlockSpec((tm, lowrank[1]), lambda i, j: (i, 0)))
        out_shapes.append(jax.ShapeDtypeStruct((n, lowrank[1]), lowrank[2]))
        start += 1
    return pl.pallas_call(
        functools.partial(_norm_proj_body, groups=groups,
                          n_extra=len(extras) + len(row_extras),
                          lowrank_fn=None if lowrank is None else lowrank[3]),
        grid=(n // tm, start),
        in_specs=in_specs,
        out_specs=out_specs,
        out_shape=out_shapes,
        scratch_shapes=scratch,
        compiler_params=_params("parallel", "arbitrary"),
        name=name,
    )(*operands)


def _scale_ep(scale):
    return lambda acc, extra: acc * scale


def _plain_ep(acc, extra):
    return acc


def _head_norm_ep(gain_idx, scale):
    def fn(acc, extra):
        gain = extra[gain_idx][...]
        parts = []
        for h in range(acc.shape[1] // HEAD_DIM):
            a = acc[:, h * HEAD_DIM:(h + 1) * HEAD_DIM]
            parts.append(_rms(a, gain) * scale)
        return jnp.concatenate(parts, axis=-1)
    return fn


def _mem_attn_ep(gain_idx, mk_idx, mv_idx):
    norm = _head_norm_ep(gain_idx, HEAD_DIM ** -0.5)

    def fn(acc, extra):
        qm = norm(acc, extra).astype(BF16)
        mk_ref, mv_ref = extra[mk_idx], extra[mv_idx]
        parts = []
        for h in range(MEM_HEADS):
            hs = slice(h * HEAD_DIM, (h + 1) * HEAD_DIM)
            s = lax.dot_general(qm[:, hs], mk_ref[:, hs], (((1,), (1,)), ((), ())),
                                preferred_element_type=F32)
            e = jnp.exp(s - jnp.max(s, axis=1, keepdims=True))
            p = e / jnp.sum(e, axis=1, keepdims=True)
            parts.append(jnp.dot(p.astype(BF16), mv_ref[:, hs], preferred_element_type=F32))
        return jnp.concatenate(parts, axis=-1)
    return fn


def _gla_gate_ep(w_up_idx, bias_idx):
    def fn(lr, extra):
        z = jnp.dot(lr, extra[w_up_idx][...], preferred_element_type=F32)
        return _log_sigmoid(z + extra[bias_idx][...]) / GLA_GATE_NORMALIZER
    return fn


def _forget_gate_ep(bias_idx):
    def fn(acc, extra):
        return _log_sigmoid(acc[:, :LANES] + extra[bias_idx][...])
    return fn


def _cumsum_body(lf_ref, col_ref, kd_ref, carry):
    @pl.when(pl.program_id(1) == 0)
    def _():
        carry[...] = jnp.zeros_like(carry)

    cum = _cumsum_rows(lf_ref[...]) + carry[...]
    carry[...] = cum[cum.shape[0] - 1:, :]
    c2 = cum * LOG2E
    col_ref[...] = c2
    terms = [term.astype(F32) for term in _split3(-c2)]
    lane = lax.broadcasted_iota(jnp.int32, c2.shape, 1)
    for h in range(kd_ref.shape[0]):
        tile = jnp.zeros_like(c2)
        for i, term in enumerate(terms):
            tile = jnp.where(lane == i, term[:, h:h + 1], tile)
        kd_ref[h] = tile.astype(BF16)


def _seq_cumsum(logf, bsz, seq, heads, *, ts):
    return pl.pallas_call(
        _cumsum_body,
        grid=(bsz, seq // ts),
        in_specs=[pl.BlockSpec((None, ts, LANES), lambda b, s: (b, s, 0))],
        out_specs=[pl.BlockSpec((None, ts, LANES), lambda b, s: (b, s, 0)),
                   pl.BlockSpec((None, heads, ts, LANES), lambda b, s: (b, 0, s, 0))],
        out_shape=[jax.ShapeDtypeStruct((bsz, seq, LANES), F32),
                   jax.ShapeDtypeStruct((bsz, heads, seq, LANES), BF16)],
        scratch_shapes=[pltpu.VMEM((1, LANES), F32)],
        compiler_params=_params("parallel", "arbitrary"),
        name="forget_cumsum",
    )(logf.reshape(bsz, seq, LANES))


def _gla_body(q_ref, k_ref, la_ref, v_ref, g_ref, gain_ref, o_ref, st_ref, *, n_chunk, hv):
    @pl.when(pl.program_id(1) == 0)
    def _():
        st_ref[...] = jnp.zeros_like(st_ref)

    gain = gain_ref[...]
    for c in range(n_chunk):
        rows = pl.ds(c * CHUNK, CHUNK)
        cum = _cumsum_rows(la_ref[rows, :])
        total = cum[CHUNK - 1:, :]
        k_dec = (k_ref[rows, :] * jnp.exp(total - cum)).astype(BF16)
        a_chunk = jnp.exp(total)
        for h in range(GLA_HEADS):
            ks = slice(h * GLA_HK_PAD, (h + 1) * GLA_HK_PAD)
            vs = slice(h * hv, (h + 1) * hv)
            upd = lax.dot_general(v_ref[rows, vs], k_dec[:, ks],
                                  (((0,), (0,)), ((), ())), preferred_element_type=F32)
            st = a_chunk[:, ks] * st_ref[h] + upd
            st_ref[h] = st
            o = lax.dot_general(q_ref[rows, ks], st.astype(BF16),
                                (((1,), (1,)), ((), ())), preferred_element_type=F32)
            o = _rms(o, gain) * _silu(g_ref[rows, vs].astype(F32))
            o_ref[rows, vs] = o.astype(BF16)


def _gla(q, k, la, v, g, gain, bsz, seq, *, n_chunk):
    hv = v.shape[1] // GLA_HEADS
    kw = q.shape[1]
    tc = n_chunk * CHUNK
    nt = seq // tc

    def spec(width):
        return pl.BlockSpec((tc, width), lambda b, c: (b * nt + c, 0))

    return pl.pallas_call(
        functools.partial(_gla_body, n_chunk=n_chunk, hv=hv),
        grid=(bsz, nt),
        in_specs=[spec(kw), spec(kw), spec(kw), spec(v.shape[1]), spec(v.shape[1]),
                  pl.BlockSpec((1, hv), lambda b, c: (0, 0))],
        out_specs=spec(v.shape[1]),
        out_shape=jax.ShapeDtypeStruct(v.shape, BF16),
        scratch_shapes=[pltpu.VMEM((GLA_HEADS, hv, GLA_HK_PAD), F32)],
        compiler_params=_params("parallel", "arbitrary"),
        name="gla",
    )(q, k, la, v, g, gain.reshape(1, hv))


def _fox_body(qi_tab, ki_tab, q_ref, k_ref, v_ref, vprev_ref, kd_ref, ccol_ref, g_ref, o_ref,
              qa_scr, ka_scr, p_scr, m_scr, l_scr, acc_scr, cc_scr, *, t, rq, hp):
    step = pl.program_id(2)
    head0 = pl.program_id(1) * hp
    qi = qi_tab[step]
    ki = ki_tab[step]

    def start_query_tile():
        m_scr[...] = jnp.full_like(m_scr, -jnp.inf)
        l_scr[...] = jnp.zeros_like(l_scr)
        acc_scr[...] = jnp.zeros_like(acc_scr)
        blk = ccol_ref[...]
        lane = lax.broadcasted_iota(jnp.int32, blk.shape, 1)
        for j in range(hp):
            cc_scr[j] = jnp.sum(jnp.where(lane == head0 + j, blk, 0.0), axis=1, keepdims=True)
            qa_scr[j, :, :HEAD_DIM] = q_ref[:, j * HEAD_DIM:(j + 1) * HEAD_DIM]
            qa_scr[j, :, HEAD_DIM:] = (lane < DECAY_TERMS).astype(BF16)

    def load_keys():
        for j in range(hp):
            ka_scr[j, :, :HEAD_DIM] = k_ref[:, j * HEAD_DIM:(j + 1) * HEAD_DIM]
            ka_scr[j, :, HEAD_DIM:] = kd_ref[j]

    def update(j, r, n_cols, masked, pending):
        rows = pl.ds(r * rq, rq)
        u = lax.dot_general(qa_scr[j, rows, :], ka_scr[j, :n_cols, :], (((1,), (1,)), ((), ())),
                            preferred_element_type=F32)
        if masked:
            tail = u[:, n_cols - rq:]
            row = lax.broadcasted_iota(jnp.int32, tail.shape, 0)
            col = lax.broadcasted_iota(jnp.int32, tail.shape, 1)
            tail = jnp.where(col <= row, tail, -jnp.inf)
            u = tail if n_cols == rq else jnp.concatenate([u[:, :n_cols - rq], tail], axis=1)
        cc = cc_scr[j, rows, :]
        m_prev = m_scr[j, rows, :]
        m_new = jnp.maximum(m_prev, jnp.max(u, axis=1, keepdims=True) + cc)
        p = jnp.exp2(u + (cc - m_new))
        alpha = jnp.exp2(m_prev - m_new)
        l_scr[j, rows, :] = alpha * l_scr[j, rows, :] + jnp.sum(p, axis=1, keepdims=True)
        m_scr[j, rows, :] = m_new
        acc = acc_scr[j, rows, :]
        if pending:
            acc = acc + jnp.dot(p_scr[j, rows, :], vprev_ref[:, j * HEAD_DIM:(j + 1) * HEAD_DIM],
                                preferred_element_type=F32)
        acc_scr[j, rows, :] = alpha * acc
        p_scr[j, rows, :n_cols] = p.astype(BF16)

    n_sub = t // rq

    def tile(masked, pending):
        for r in range(n_sub):
            for j in range(hp):
                update(j, r, (r + 1) * rq if masked else t, masked, pending)

    for diag in (False, True):
        for pending in (False, True):
            first = ki == 0

            @pl.when(((ki == qi) if diag else (ki < qi)) & (~first if pending else first))
            def _(diag=diag, pending=pending):
                if not pending:
                    start_query_tile()
                load_keys()
                tile(diag, pending)

    @pl.when(ki == qi)
    def _():
        for j in range(hp):
            hs = slice(j * HEAD_DIM, (j + 1) * HEAD_DIM)
            for r in range(n_sub):
                rows = pl.ds(r * rq, rq)
                n_cols = (r + 1) * rq
                acc_scr[j, rows, :] += jnp.dot(p_scr[j, rows, :n_cols], v_ref[:n_cols, hs],
                                               preferred_element_type=F32)
            o = acc_scr[j] / l_scr[j]
            o_ref[:, hs] = (o * jax.nn.sigmoid(g_ref[:, hs].astype(F32))).astype(BF16)


def _fox(q, k, v, kd, c2, g, *, t, rq, hp):
    bsz, seq, width = q.shape
    heads = width // HEAD_DIM
    hw = hp * HEAD_DIM
    pairs = [(qi, ki) for qi in range(seq // t) for ki in range(qi + 1)]
    qi_tab = jnp.asarray(np.array([p[0] for p in pairs], np.int32))
    ki_tab = jnp.asarray(np.array([p[1] for p in pairs], np.int32))
    q_spec = pl.BlockSpec((None, t, hw), lambda b, h, s, qt, kt: (b, qt[s], h))
    kv_spec = pl.BlockSpec((None, t, hw), lambda b, h, s, qt, kt: (b, kt[s], h))
    vprev_spec = pl.BlockSpec((None, t, hw),
                              lambda b, h, s, qt, kt: (b, jnp.maximum(kt[s] - 1, 0), h))
    grid_spec = pltpu.PrefetchScalarGridSpec(
        num_scalar_prefetch=2,
        grid=(bsz, heads // hp, len(pairs)),
        in_specs=[
            q_spec, kv_spec, kv_spec, vprev_spec,
            pl.BlockSpec((None, hp, t, LANES), lambda b, h, s, qt, kt: (b, h, kt[s], 0)),
            pl.BlockSpec((None, t, LANES), lambda b, h, s, qt, kt: (b, qt[s], 0)),
            q_spec,
        ],
        out_specs=q_spec,
        scratch_shapes=[pltpu.VMEM((hp, t, 2 * HEAD_DIM), BF16),
                        pltpu.VMEM((hp, t, 2 * HEAD_DIM), BF16),
                        pltpu.VMEM((hp, t, t), BF16),
                        pltpu.VMEM((hp, t, 1), F32), pltpu.VMEM((hp, t, 1), F32),
                        pltpu.VMEM((hp, t, HEAD_DIM), F32), pltpu.VMEM((hp, t, 1), F32)],
    )
    return pl.pallas_call(
        functools.partial(_fox_body, t=t, rq=rq, hp=hp),
        grid_spec=grid_spec,
        out_shape=jax.ShapeDtypeStruct(q.shape, BF16),
        compiler_params=_params("parallel", "parallel", "arbitrary"),
        name="fox",
    )(qi_tab, ki_tab, q, k, v, v, kd, c2, g)


def _out_proj_body(x_ref, o_ref, mo_ref, wo_ref, wm_ref, y_ref):
    y = x_ref[...] + jnp.dot(o_ref[...], wo_ref[...], preferred_element_type=F32)
    y_ref[...] = y + jnp.dot(mo_ref[...], wm_ref[...], preferred_element_type=F32)


def _out_proj(x, o, mo, w_out, layer, *, tm):
    n, d = x.shape
    main_w = o.shape[1]
    assert main_w % MEM_W == 0 and mo.shape[1] == MEM_W
    return pl.pallas_call(
        _out_proj_body,
        grid=(n // tm,),
        in_specs=[pl.BlockSpec((tm, d), lambda i: (i, 0)),
                  pl.BlockSpec((tm, main_w), lambda i: (i, 0)),
                  pl.BlockSpec((tm, MEM_W), lambda i: (i, 0)),
                  pl.BlockSpec((None, main_w, d), lambda i: (layer, 0, 0)),
                  pl.BlockSpec((None, MEM_W, d), lambda i: (layer, main_w // MEM_W, 0))],
        out_specs=pl.BlockSpec((tm, d), lambda i: (i, 0)),
        out_shape=jax.ShapeDtypeStruct((n, d), F32),
        compiler_params=_params("parallel"),
        name="out_proj",
    )(x, o, mo, w_out, w_out)


def _pad_cols(w, width):
    return jnp.pad(w, [(0, 0)] * (w.ndim - 1) + [(0, width - w.shape[-1])])


def _pad_gla_heads(w, hk):
    lead = w.shape[:-1]
    w = w.reshape(lead + (GLA_HEADS, hk))
    w = jnp.pad(w, [(0, 0)] * len(lead) + [(0, 0), (0, GLA_HK_PAD - hk)])
    return w.reshape(lead + (GLA_HEADS * GLA_HK_PAD,))


def _forward(x, mem, ffn_norm, ffn_w1, ffn_w3, ffn_w2, mix_norm, mem_norm, w_mem_kv,
             mem_q_norm, mem_k_norm, w_out, a_w_in, a_w_gate_up, a_b_gate, a_out_norm,
             b_w_in, b_q_norm, kv_norm, w_kv, b_f, k_norm, *, tiles):
    bsz, seq, d = x.shape
    n = bsz * seq
    depth = ffn_norm.shape[0]
    n_a = a_w_in.shape[0]
    main_w = d - MEM_W
    gla_key = main_w // 2
    hk = gla_key // GLA_HEADS
    fox_heads = main_w // HEAD_DIM
    tn = tiles["proj_tn"]
    attn_scale = HEAD_DIM ** -0.5

    w1 = ffn_w1.astype(BF16)
    w3 = ffn_w3.astype(BF16)
    w2 = ffn_w2.astype(BF16)
    w_out_b = w_out.astype(BF16)
    w_mem_b = w_mem_kv.astype(BF16)
    w_b = b_w_in.astype(BF16)
    w_kv_b = jnp.concatenate([w_kv[:, :2 * main_w], _pad_cols(w_kv[:, 2 * main_w:], tn)],
                             axis=1).astype(BF16)[None]
    c = np.cumsum([0, gla_key, gla_key, main_w, GLA_GATE_RANK, main_w, MEM_W])
    kw = GLA_HEADS * GLA_HK_PAD
    w_a = jnp.concatenate([
        _pad_gla_heads(a_w_in[..., c[0]:c[1]], hk),
        _pad_gla_heads(a_w_in[..., c[1]:c[2]], hk),
        a_w_in[..., c[2]:c[3]],
        a_w_in[..., c[4]:c[5]],
        a_w_in[..., c[5]:c[6]],
    ], axis=-1).astype(BF16)
    w_lr = _pad_cols(a_w_in[..., c[3]:c[4]], LANES).astype(BF16)
    w_up = jnp.pad(_pad_gla_heads(a_w_gate_up, hk),
                   ((0, 0), (0, LANES - GLA_GATE_RANK), (0, 0))).astype(BF16)
    b_gate = _pad_gla_heads(a_b_gate, hk)[:, None, :]
    x2 = x.reshape(n, d)
    mem2 = mem.reshape(bsz * mem.shape[1], d)

    def ffn(x2, l, s):
        return _ffn(x2, ffn_norm[l, s], w1, w3, w2, l, s,
                    tm=tiles["ffn_tm"], tf=tiles["ffn_tf"])

    k_sh = v_sh = c2 = k_decay = None
    for l in range(depth):
        if l == n_a:
            bias = _pad_cols(b_f.reshape(1, fox_heads), LANES)
            k_sh, v_sh, logf = _norm_proj(
                x2, kv_norm, w_kv_b, 0,
                [(main_w // tn, tn, BF16, _head_norm_ep(0, 1.0)),
                 (main_w // tn, tn, BF16, _plain_ep),
                 (1, LANES, F32, _forget_gate_ep(1))],
                [k_norm.reshape(1, HEAD_DIM), bias],
                tm=tiles["proj_tm"], tn=tn, name="kv_proj")
            c2, k_decay = _seq_cumsum(logf, bsz, seq, fox_heads, ts=tiles["cum_ts"])

        x2 = ffn(x2, l, 0)

        mk, mv = _norm_proj(
            mem2, mem_norm[l], w_mem_b, l,
            [(MEM_W // tn, tn, BF16, _head_norm_ep(0, 1.0)),
             (MEM_W // tn, tn, BF16, _plain_ep)],
            [mem_k_norm[l].reshape(1, HEAD_DIM)],
            tm=tiles["mem_tm"], tn=tn, name="mem_kv_proj")
        assert MEM_W == tn, "the memory-attention epilogue needs all memory heads in one tile"
        mem_rows = [(mk, mem.shape[1], seq // tiles["proj_tm"]),
                    (mv, mem.shape[1], seq // tiles["proj_tm"])]

        if l < n_a:
            q, k, v, g, mo, la = _norm_proj(
                x2, mix_norm[l], w_a, l,
                [(kw // tn, tn, BF16, _scale_ep(hk ** -0.5)),
                 (kw // tn, tn, F32, _plain_ep),
                 (main_w // tn, tn, BF16, _plain_ep),
                 (main_w // tn, tn, BF16, _plain_ep),
                 (1, tn, BF16, _mem_attn_ep(0, 3, 4))],
                [mem_q_norm[l].reshape(1, HEAD_DIM), w_up[l], b_gate[l]],
                tm=tiles["proj_tm"], tn=tn, name="gla_in_proj", row_extras=mem_rows,
                lowrank=(w_lr[l], kw, F32, _gla_gate_ep(1, 2)))
            o = _gla(q, k, la, v, g, a_out_norm[l], bsz, seq, n_chunk=tiles["gla_chunks"])
        else:
            jl = l - n_a
            q, g, mo = _norm_proj(
                x2, mix_norm[l], w_b, jl,
                [(main_w // tn, tn, BF16, _head_norm_ep(0, attn_scale * LOG2E)),
                 (main_w // tn, tn, BF16, _plain_ep),
                 (1, tn, BF16, _mem_attn_ep(1, 2, 3))],
                [b_q_norm[jl].reshape(1, HEAD_DIM), mem_q_norm[l].reshape(1, HEAD_DIM)],
                tm=tiles["proj_tm"], tn=tn, name="fox_in_proj", row_extras=mem_rows)
            o = _fox(q.reshape(bsz, seq, main_w), k_sh.reshape(bsz, seq, main_w),
                     v_sh.reshape(bsz, seq, main_w), k_decay, c2,
                     g.reshape(bsz, seq, main_w), t=tiles["fox_t"], rq=tiles["fox_rq"], hp=tiles["fox_hp"])
            o = o.reshape(n, main_w)

        x2 = _out_proj(x2, o, mo, w_out_b, l, tm=tiles["out_tm"])
        x2 = ffn(x2, l, 1)
    return x2.reshape(bsz, seq, d)


_TILES = dict(ffn_tm=1024, ffn_tf=512, proj_tm=1024, proj_tn=512, mem_tm=512, cum_ts=512,
              gla_chunks=16, fox_t=1024, fox_rq=256, fox_hp=4, out_tm=512)


def kernel(x, mem, ffn_norm, ffn_w1, ffn_w3, ffn_w2, mix_norm, mem_norm, w_mem_kv, mem_q_norm, mem_k_norm, w_out, a_w_in, a_w_gate_up, a_b_gate, a_out_norm, b_w_in, b_q_norm, kv_norm, w_kv, b_f, k_norm):
    return _forward(x, mem, ffn_norm, ffn_w1, ffn_w3, ffn_w2, mix_norm, mem_norm, w_mem_kv,
                    mem_q_norm, mem_k_norm, w_out, a_w_in, a_w_gate_up, a_b_gate, a_out_norm,
                    b_w_in, b_q_norm, kv_norm, w_kv, b_f, k_norm, tiles=_TILES)
```

```python
import functools

import numpy as np
import jax
import jax.numpy as jnp
from jax import lax
from jax.experimental import pallas as pl
from jax.experimental.pallas import tpu as pltpu

F32 = jnp.float32
BF16 = jnp.bfloat16

CHUNK = 64
HEAD_DIM = 128
MEM_HEADS = 4
MEM_W = MEM_HEADS * HEAD_DIM
GLA_HEADS = 4
GLA_GATE_RANK = 16
GLA_GATE_NORMALIZER = 16.0
EPS = 1e-6
LOG2E = 1.4426950408889634
DECAY_TERMS = 3

LANES = 128
V7X_VMEM_LIMIT_BYTES = 56 * 1024 * 1024

GLA_HK_PAD = 256


def _rms(x, gain):
    return x * lax.rsqrt(jnp.mean(x * x, axis=-1, keepdims=True) + EPS) * gain


def _log_sigmoid(x):
    return jnp.minimum(x, 0.0) - jnp.log1p(jnp.exp(-jnp.abs(x)))


def _silu(x):
    return x * jax.nn.sigmoid(x)


def _split3(x):
    hi = x.astype(BF16)
    r1 = x - hi.astype(F32)
    mid = r1.astype(BF16)
    lo = (r1 - mid.astype(F32)).astype(BF16)
    return hi, mid, lo


def _cumsum_rows(x):
    n = x.shape[0]
    row = lax.broadcasted_iota(jnp.int32, (n, n), 0)
    col = lax.broadcasted_iota(jnp.int32, (n, n), 1)
    tri = (row >= col).astype(BF16)
    return sum(jnp.dot(tri, term, preferred_element_type=F32) for term in _split3(x))


def _params(*semantics):
    return pltpu.CompilerParams(dimension_semantics=semantics,
                                vmem_limit_bytes=V7X_VMEM_LIMIT_BYTES)


def _ffn_body(x_ref, g_ref, w1_ref, w3_ref, w2_ref, o_ref, h_scr):
    def ff_tile(h):
        a = jnp.dot(h, w1_ref[...], preferred_element_type=F32)
        b = jnp.dot(h, w3_ref[...], preferred_element_type=F32)
        act = (_silu(a) * b * 0.5).astype(BF16)
        return jnp.dot(act, w2_ref[...], preferred_element_type=F32)

    @pl.when(pl.program_id(1) == 0)
    def _():
        x = x_ref[...]
        h = _rms(x, g_ref[...]).astype(BF16)
        h_scr[...] = h
        o_ref[...] = x + ff_tile(h)

    @pl.when(pl.program_id(1) > 0)
    def _():
        o_ref[...] += ff_tile(h_scr[...])


def _ffn(x, gain, w1, w3, w2, layer, half, *, tm, tf):
    n, d = x.shape
    f = w1.shape[-1]
    return pl.pallas_call(
        _ffn_body,
        grid=(n // tm, f // tf),
        in_specs=[
            pl.BlockSpec((tm, d), lambda i, j: (i, 0)),
            pl.BlockSpec((1, d), lambda i, j: (0, 0)),
            pl.BlockSpec((None, None, d, tf), lambda i, j: (layer, half, 0, j)),
            pl.BlockSpec((None, None, d, tf), lambda i, j: (layer, half, 0, j)),
            pl.BlockSpec((None, None, tf, d), lambda i, j: (layer, half, j, 0)),
        ],
        out_specs=pl.BlockSpec((tm, d), lambda i, j: (i, 0)),
        out_shape=jax.ShapeDtypeStruct((n, d), F32),
        scratch_shapes=[pltpu.VMEM((tm, d), BF16)],
        compiler_params=_params("parallel", "arbitrary"),
        name="ffn",
    )(x, gain.reshape(1, d), w1, w3, w2)


def _norm_proj_body(*refs, groups, n_extra, lowrank_fn):
    x_ref, g_ref, w_ref = refs[:3]
    n_in = 3 + (lowrank_fn is not None)
    extra = refs[n_in:n_in + n_extra]
    outs = refs[n_in + n_extra:n_in + n_extra + len(groups)]
    if lowrank_fn is None:
        h_scr = refs[-1]
    else:
        w_lr_ref, lr_out_ref, h_scr, lr_scr = refs[3], refs[-3], refs[-2], refs[-1]
    j = pl.program_id(1)

    def tile(h, fn, o_ref):
        acc = jnp.dot(h, w_ref[...], preferred_element_type=F32)
        o_ref[...] = fn(acc, extra).astype(o_ref.dtype)

    @pl.when(j == 0)
    def _():
        h = _rms(x_ref[...], g_ref[...]).astype(BF16)
        h_scr[...] = h
        if lowrank_fn is not None:
            lr_scr[...] = jnp.dot(h, w_lr_ref[...], preferred_element_type=F32).astype(BF16)
        tile(h, groups[0][3], outs[0])

    start = 0
    for (n_tiles, _, _, fn), o_ref in zip(groups, outs):
        @pl.when((j >= max(start, 1)) & (j < start + n_tiles))
        def _(fn=fn, o_ref=o_ref):
            tile(h_scr[...], fn, o_ref)
        start += n_tiles

    if lowrank_fn is not None:
        @pl.when(j == start)
        def _():
            lr_out_ref[...] = lowrank_fn(lr_scr[...], extra).astype(lr_out_ref.dtype)


def _norm_proj(x, gain, w, layer, groups, extras, *, tm, tn, name, row_extras=(), lowrank=None):
    n, d = x.shape
    n_w_tiles = sum(g[0] for g in groups)
    assert w.shape[2] == tn * n_w_tiles
    in_specs = [
        pl.BlockSpec((tm, d), lambda i, j: (i, 0)),
        pl.BlockSpec((1, d), lambda i, j: (0, 0)),
        pl.BlockSpec((None, d, tn), lambda i, j: (layer, 0, jnp.minimum(j, n_w_tiles - 1))),
    ]
    operands = [x, gain.reshape(1, d), w]
    scratch = [pltpu.VMEM((tm, d), BF16)]
    if lowrank is not None:
        in_specs.append(pl.BlockSpec(lowrank[0].shape, lambda i, j: (0, 0)))
        operands.append(lowrank[0])
        scratch.append(pltpu.VMEM((tm, lowrank[0].shape[1]), BF16))
    in_specs += [pl.BlockSpec(e.shape, lambda i, j: (0, 0)) for e in extras] + [
        pl.BlockSpec((rows, a.shape[1]), lambda i, j, per=per: (i // per, 0))
        for a, rows, per in row_extras]
    operands += list(extras) + [a for a, _, _ in row_extras]
    out_specs, out_shapes = [], []
    start = 0
    for n_tiles, width, dtype, _ in groups:
        out_specs.append(pl.BlockSpec(
            (tm, width),
            lambda i, j, s=start, c=n_tiles: (i, jnp.clip(j - s, 0, c - 1))))
        out_shapes.append(jax.ShapeDtypeStruct((n, width * n_tiles), dtype))
        start += n_tiles
    if lowrank is not None:
        out_specs.append(pl.BlockSpec((tm, lowrank[1]), lambda i, j: (i, 0)))
        out_shapes.append(jax.ShapeDtypeStruct((n, lowrank[1]), lowrank[2]))
        start += 1
    return pl.pallas_call(
        functools.partial(_norm_proj_body, groups=groups,
                          n_extra=len(extras) + len(row_extras),
                          lowrank_fn=None if lowrank is None else lowrank[3]),
        grid=(n // tm, start),
        in_specs=in_specs,
        out_specs=out_specs,
        out_shape=out_shapes,
        scratch_shapes=scratch,
        compiler_params=_params("parallel", "arbitrary"),
        name=name,
    )(*operands)


def _scale_ep(scale):
    return lambda acc, extra: acc * scale


def _plain_ep(acc, extra):
    return acc


def _head_norm_ep(gain_idx, scale):
    def fn(acc, extra):
        gain = extra[gain_idx][...]
        parts = []
        for h in range(acc.shape[1] // HEAD_DIM):
            a = acc[:, h * HEAD_DIM:(h + 1) * HEAD_DIM]
            parts.append(_rms(a, gain) * scale)
        return jnp.concatenate(parts, axis=-1)
    return fn


def _mem_attn_ep(gain_idx, mk_idx, mv_idx):
    norm = _head_norm_ep(gain_idx, HEAD_DIM ** -0.5)

    def fn(acc, extra):
        qm = norm(acc, extra).astype(BF16)
        mk_ref, mv_ref = extra[mk_idx], extra[mv_idx]
        parts = []
        for h in range(MEM_HEADS):
            hs = slice(h * HEAD_DIM, (h + 1) * HEAD_DIM)
            s = lax.dot_general(qm[:, hs], mk_ref[:, hs], (((1,), (1,)), ((), ())),
                                preferred_element_type=F32)
            e = jnp.exp(s - jnp.max(s, axis=1, keepdims=True))
            p = e / jnp.sum(e, axis=1, keepdims=True)
            parts.append(jnp.dot(p.astype(BF16), mv_ref[:, hs], preferred_element_type=F32))
        return jnp.concatenate(parts, axis=-1)
    return fn


def _gla_gate_ep(w_up_idx, bias_idx):
    def fn(lr, extra):
        z = jnp.dot(lr, extra[w_up_idx][...], preferred_element_type=F32)
        return _log_sigmoid(z + extra[bias_idx][...]) / GLA_GATE_NORMALIZER
    return fn


def _forget_gate_ep(bias_idx):
    def fn(acc, extra):
        return _log_sigmoid(acc[:, :LANES] + extra[bias_idx][...])
    return fn


def _cumsum_body(lf_ref, col_ref, kd_ref, carry):
    @pl.when(pl.program_id(1) == 0)
    def _():
        carry[...] = jnp.zeros_like(carry)

    cum = _cumsum_rows(lf_ref[...]) + carry[...]
    carry[...] = cum[cum.shape[0] - 1:, :]
    c2 = cum * LOG2E
    col_ref[...] = c2
    terms = [term.astype(F32) for term in _split3(-c2)]
    lane = lax.broadcasted_iota(jnp.int32, c2.shape, 1)
    for h in range(kd_ref.shape[0]):
        tile = jnp.zeros_like(c2)
        for i, term in enumerate(terms):
            tile = jnp.where(lane == i, term[:, h:h + 1], tile)
        kd_ref[h] = tile.astype(BF16)


def _seq_cumsum(logf, bsz, seq, heads, *, ts):
    return pl.pallas_call(
        _cumsum_body,
        grid=(bsz, seq // ts),
        in_specs=[pl.BlockSpec((None, ts, LANES), lambda b, s: (b, s, 0))],
        out_specs=[pl.BlockSpec((None, ts, LANES), lambda b, s: (b, s, 0)),
                   pl.BlockSpec((None, heads, ts, LANES), lambda b, s: (b, 0, s, 0))],
        out_shape=[jax.ShapeDtypeStruct((bsz, seq, LANES), F32),
                   jax.ShapeDtypeStruct((bsz, heads, seq, LANES), BF16)],
        scratch_shapes=[pltpu.VMEM((1, LANES), F32)],
        compiler_params=_params("parallel", "arbitrary"),
        name="forget_cumsum",
    )(logf.reshape(bsz, seq, LANES))


def _gla_body(q_ref, k_ref, la_ref, v_ref, g_ref, gain_ref, o_ref, st_ref, *, n_chunk, hv):
    @pl.when(pl.program_id(1) == 0)
    def _():
        st_ref[...] = jnp.zeros_like(st_ref)

    gain = gain_ref[...]
    for c in range(n_chunk):
        rows = pl.ds(c * CHUNK, CHUNK)
        cum = _cumsum_rows(la_ref[rows, :])
        total = cum[CHUNK - 1:, :]
        k_dec = (k_ref[rows, :] * jnp.exp(total - cum)).astype(BF16)
        a_chunk = jnp.exp(total)
        for h in range(GLA_HEADS):
            ks = slice(h * GLA_HK_PAD, (h + 1) * GLA_HK_PAD)
            vs = slice(h * hv, (h + 1) * hv)
            upd = lax.dot_general(v_ref[rows, vs], k_dec[:, ks],
                                  (((0,), (0,)), ((), ())), preferred_element_type=F32)
            st = a_chunk[:, ks] * st_ref[h] + upd
            st_ref[h] = st
            o = lax.dot_general(q_ref[rows, ks], st.astype(BF16),
                                (((1,), (1,)), ((), ())), preferred_element_type=F32)
            o = _rms(o, gain) * _silu(g_ref[rows, vs].astype(F32))
            o_ref[rows, vs] = o.astype(BF16)


def _gla(q, k, la, v, g, gain, bsz, seq, *, n_chunk):
    hv = v.shape[1] // GLA_HEADS
    kw = q.shape[1]
    tc = n_chunk * CHUNK
    nt = seq // tc

    def spec(width):
        return pl.BlockSpec((tc, width), lambda b, c: (b * nt + c, 0))

    return pl.pallas_call(
        functools.partial(_gla_body, n_chunk=n_chunk, hv=hv),
        grid=(bsz, nt),
        in_specs=[spec(kw), spec(kw), spec(kw), spec(v.shape[1]), spec(v.shape[1]),
                  pl.BlockSpec((1, hv), lambda b, c: (0, 0))],
        out_specs=spec(v.shape[1]),
        out_shape=jax.ShapeDtypeStruct(v.shape, BF16),
        scratch_shapes=[pltpu.VMEM((GLA_HEADS, hv, GLA_HK_PAD), F32)],
        compiler_params=_params("parallel", "arbitrary"),
        name="gla",
    )(q, k, la, v, g, gain.reshape(1, hv))


def _fox_body(qi_tab, ki_tab, q_ref, k_ref, v_ref, vprev_ref, kd_ref, ccol_ref, g_ref, o_ref,
              qa_scr, ka_scr, va_scr, vpa_scr, p_scr, m_scr, accl_scr, cc_scr, *, t, rq, hp):
    step = pl.program_id(2)
    head0 = pl.program_id(1) * hp
    qi = qi_tab[step]
    ki = ki_tab[step]

    def start_query_tile():
        m_scr[...] = jnp.full_like(m_scr, -jnp.inf)
        accl_scr[...] = jnp.zeros_like(accl_scr)
        blk = ccol_ref[...]
        lane = lax.broadcasted_iota(jnp.int32, blk.shape, 1)
        for j in range(hp):
            va_scr[j, :, HEAD_DIM:] = (lane == 0).astype(BF16)
            vpa_scr[j, :, HEAD_DIM:] = (lane == 0).astype(BF16)
            cc_scr[j] = jnp.sum(jnp.where(lane == head0 + j, blk, 0.0), axis=1, keepdims=True)
            qa_scr[j, :, :HEAD_DIM] = q_ref[:, j * HEAD_DIM:(j + 1) * HEAD_DIM]
            qa_scr[j, :, HEAD_DIM:] = (lane < DECAY_TERMS).astype(BF16)

    def load_keys(pending):
        for j in range(hp):
            hs = slice(j * HEAD_DIM, (j + 1) * HEAD_DIM)
            ka_scr[j, :, :HEAD_DIM] = k_ref[:, hs]
            ka_scr[j, :, HEAD_DIM:] = kd_ref[j]
            if pending:
                vpa_scr[j, :, :HEAD_DIM] = vprev_ref[:, hs]

    def update(j, r, n_cols, masked, pending):
        rows = pl.ds(r * rq, rq)
        u = lax.dot_general(qa_scr[j, rows, :], ka_scr[j, :n_cols, :], (((1,), (1,)), ((), ())),
                            preferred_element_type=F32)
        if masked:
            tail = u[:, n_cols - rq:]
            row = lax.broadcasted_iota(jnp.int32, tail.shape, 0)
            col = lax.broadcasted_iota(jnp.int32, tail.shape, 1)
            tail = jnp.where(col <= row, tail, -jnp.inf)
            u = tail if n_cols == rq else jnp.concatenate([u[:, :n_cols - rq], tail], axis=1)
        cc = cc_scr[j, rows, :]
        m_prev = m_scr[j, rows, :]
        m_new = jnp.maximum(m_prev, jnp.max(u, axis=1, keepdims=True) + cc)
        p = jnp.exp2(u + (cc - m_new))
        alpha = jnp.exp2(m_prev - m_new)
        m_scr[j, rows, :] = m_new
        accl = accl_scr[j, rows, :]
        if pending:
            accl = accl + jnp.dot(p_scr[j, rows, :], vpa_scr[j], preferred_element_type=F32)
        accl_scr[j, rows, :] = alpha * accl
        p_scr[j, rows, :n_cols] = p.astype(BF16)

    n_sub = t // rq

    def tile(masked, pending):
        for r in range(n_sub):
            for j in range(hp):
                update(j, r, (r + 1) * rq if masked else t, masked, pending)

    for diag in (False, True):
        for pending in (False, True):
            first = ki == 0

            @pl.when(((ki == qi) if diag else (ki < qi)) & (~first if pending else first))
            def _(diag=diag, pending=pending):
                if not pending:
                    start_query_tile()
                load_keys(pending)
                tile(diag, pending)

    @pl.when(ki == qi)
    def _():
        for j in range(hp):
            hs = slice(j * HEAD_DIM, (j + 1) * HEAD_DIM)
            va_scr[j, :, :HEAD_DIM] = v_ref[:, hs]
            for r in range(n_sub):
                rows = pl.ds(r * rq, rq)
                n_cols = (r + 1) * rq
                accl_scr[j, rows, :] += jnp.dot(p_scr[j, rows, :n_cols], va_scr[j, :n_cols, :],
                                                preferred_element_type=F32)
            accl = accl_scr[j]
            o = accl[:, :HEAD_DIM] / accl[:, HEAD_DIM:HEAD_DIM + 1]
            o_ref[:, hs] = (o * jax.nn.sigmoid(g_ref[:, hs].astype(F32))).astype(BF16)


def _fox(q, k, v, kd, c2, g, *, t, rq, hp):
    bsz, seq, width = q.shape
    heads = width // HEAD_DIM
    hw = hp * HEAD_DIM
    pairs = [(qi, ki) for qi in range(seq // t) for ki in range(qi + 1)]
    qi_tab = jnp.asarray(np.array([p[0] for p in pairs], np.int32))
    ki_tab = jnp.asarray(np.array([p[1] for p in pairs], np.int32))
    q_spec = pl.BlockSpec((None, t, hw), lambda b, h, s, qt, kt: (b, qt[s], h))
    kv_spec = pl.BlockSpec((None, t, hw), lambda b, h, s, qt, kt: (b, kt[s], h))
    vprev_spec = pl.BlockSpec((None, t, hw),
                              lambda b, h, s, qt, kt: (b, jnp.maximum(kt[s] - 1, 0), h))
    grid_spec = pltpu.PrefetchScalarGridSpec(
        num_scalar_prefetch=2,
        grid=(bsz, heads // hp, len(pairs)),
        in_specs=[
            q_spec, kv_spec, kv_spec, vprev_spec,
            pl.BlockSpec((None, hp, t, LANES), lambda b, h, s, qt, kt: (b, h, kt[s], 0)),
            pl.BlockSpec((None, t, LANES), lambda b, h, s, qt, kt: (b, qt[s], 0)),
            q_spec,
        ],
        out_specs=q_spec,
        scratch_shapes=[pltpu.VMEM((hp, t, 2 * HEAD_DIM), BF16),
                        pltpu.VMEM((hp, t, 2 * HEAD_DIM), BF16),
                        pltpu.VMEM((hp, t, 2 * HEAD_DIM), BF16),
                        pltpu.VMEM((hp, t, 2 * HEAD_DIM), BF16),
                        pltpu.VMEM((hp, t, t), BF16),
                        pltpu.VMEM((hp, t, 1), F32),
                        pltpu.VMEM((hp, t, 2 * HEAD_DIM), F32), pltpu.VMEM((hp, t, 1), F32)],
    )
    return pl.pallas_call(
        functools.partial(_fox_body, t=t, rq=rq, hp=hp),
        grid_spec=grid_spec,
        out_shape=jax.ShapeDtypeStruct(q.shape, BF16),
        compiler_params=_params("parallel", "parallel", "arbitrary"),
        name="fox",
    )(qi_tab, ki_tab, q, k, v, v, kd, c2, g)


def _out_proj_body(x_ref, o_ref, mo_ref, wo_ref, wm_ref, y_ref):
    y = x_ref[...] + jnp.dot(o_ref[...], wo_ref[...], preferred_element_type=F32)
    y_ref[...] = y + jnp.dot(mo_ref[...], wm_ref[...], preferred_element_type=F32)


def _out_proj(x, o, mo, w_out, layer, *, tm):
    n, d = x.shape
    main_w = o.shape[1]
    assert main_w % MEM_W == 0 and mo.shape[1] == MEM_W
    return pl.pallas_call(
        _out_proj_body,
        grid=(n // tm,),
        in_specs=[pl.BlockSpec((tm, d), lambda i: (i, 0)),
                  pl.BlockSpec((tm, main_w), lambda i: (i, 0)),
                  pl.BlockSpec((tm, MEM_W), lambda i: (i, 0)),
                  pl.BlockSpec((None, main_w, d), lambda i: (layer, 0, 0)),
                  pl.BlockSpec((None, MEM_W, d), lambda i: (layer, main_w // MEM_W, 0))],
        out_specs=pl.BlockSpec((tm, d), lambda i: (i, 0)),
        out_shape=jax.ShapeDtypeStruct((n, d), F32),
        compiler_params=_params("parallel"),
        name="out_proj",
    )(x, o, mo, w_out, w_out)


def _pad_cols(w, width):
    return jnp.pad(w, [(0, 0)] * (w.ndim - 1) + [(0, width - w.shape[-1])])


def _pad_gla_heads(w, hk):
    lead = w.shape[:-1]
    w = w.reshape(lead + (GLA_HEADS, hk))
    w = jnp.pad(w, [(0, 0)] * len(lead) + [(0, 0), (0, GLA_HK_PAD - hk)])
    return w.reshape(lead + (GLA_HEADS * GLA_HK_PAD,))


def _forward(x, mem, ffn_norm, ffn_w1, ffn_w3, ffn_w2, mix_norm, mem_norm, w_mem_kv,
             mem_q_norm, mem_k_norm, w_out, a_w_in, a_w_gate_up, a_b_gate, a_out_norm,
             b_w_in, b_q_norm, kv_norm, w_kv, b_f, k_norm, *, tiles):
    bsz, seq, d = x.shape
    n = bsz * seq
    depth = ffn_norm.shape[0]
    n_a = a_w_in.shape[0]
    main_w = d - MEM_W
    gla_key = main_w // 2
    hk = gla_key // GLA_HEADS
    fox_heads = main_w // HEAD_DIM
    tn = tiles["proj_tn"]
    attn_scale = HEAD_DIM ** -0.5

    w1 = ffn_w1.astype(BF16)
    w3 = ffn_w3.astype(BF16)
    w2 = ffn_w2.astype(BF16)
    w_out_b = w_out.astype(BF16)
    w_mem_b = w_mem_kv.astype(BF16)
    w_b = b_w_in.astype(BF16)
    w_kv_b = jnp.concatenate([w_kv[:, :2 * main_w], _pad_cols(w_kv[:, 2 * main_w:], tn)],
                             axis=1).astype(BF16)[None]
    c = np.cumsum([0, gla_key, gla_key, main_w, GLA_GATE_RANK, main_w, MEM_W])
    kw = GLA_HEADS * GLA_HK_PAD
    w_a = jnp.concatenate([
        _pad_gla_heads(a_w_in[..., c[0]:c[1]], hk),
        _pad_gla_heads(a_w_in[..., c[1]:c[2]], hk),
        a_w_in[..., c[2]:c[3]],
        a_w_in[..., c[4]:c[5]],
        a_w_in[..., c[5]:c[6]],
    ], axis=-1).astype(BF16)
    w_lr = _pad_cols(a_w_in[..., c[3]:c[4]], LANES).astype(BF16)
    w_up = jnp.pad(_pad_gla_heads(a_w_gate_up, hk),
                   ((0, 0), (0, LANES - GLA_GATE_RANK), (0, 0))).astype(BF16)
    b_gate = _pad_gla_heads(a_b_gate, hk)[:, None, :]
    x2 = x.reshape(n, d)
    mem2 = mem.reshape(bsz * mem.shape[1], d)

    def ffn(x2, l, s):
        return _ffn(x2, ffn_norm[l, s], w1, w3, w2, l, s,
                    tm=tiles["ffn_tm"], tf=tiles["ffn_tf"])

    k_sh = v_sh = c2 = k_decay = None
    for l in range(depth):
        if l == n_a:
            bias = _pad_cols(b_f.reshape(1, fox_heads), LANES)
            k_sh, v_sh, logf = _norm_proj(
                x2, kv_norm, w_kv_b, 0,
                [(main_w // tn, tn, BF16, _head_norm_ep(0, 1.0)),
                 (main_w // tn, tn, BF16, _plain_ep),
                 (1, LANES, F32, _forget_gate_ep(1))],
                [k_norm.reshape(1, HEAD_DIM), bias],
                tm=tiles["proj_tm"], tn=tn, name="kv_proj")
            c2, k_decay = _seq_cumsum(logf, bsz, seq, fox_heads, ts=tiles["cum_ts"])

        x2 = ffn(x2, l, 0)

        mk, mv = _norm_proj(
            mem2, mem_norm[l], w_mem_b, l,
            [(MEM_W // tn, tn, BF16, _head_norm_ep(0, 1.0)),
             (MEM_W // tn, tn, BF16, _plain_ep)],
            [mem_k_norm[l].reshape(1, HEAD_DIM)],
            tm=tiles["mem_tm"], tn=tn, name="mem_kv_proj")
        assert MEM_W == tn, "the memory-attention epilogue needs all memory heads in one tile"
        mem_rows = [(mk, mem.shape[1], seq // tiles["proj_tm"]),
                    (mv, mem.shape[1], seq // tiles["proj_tm"])]

        if l < n_a:
            q, k, v, g, mo, la = _norm_proj(
                x2, mix_norm[l], w_a, l,
                [(kw // tn, tn, BF16, _scale_ep(hk ** -0.5)),
                 (kw // tn, tn, F32, _plain_ep),
                 (main_w // tn, tn, BF16, _plain_ep),
                 (main_w // tn, tn, BF16, _plain_ep),
                 (1, tn, BF16, _mem_attn_ep(0, 3, 4))],
                [mem_q_norm[l].reshape(1, HEAD_DIM), w_up[l], b_gate[l]],
                tm=tiles["proj_tm"], tn=tn, name="gla_in_proj", row_extras=mem_rows,
                lowrank=(w_lr[l], kw, F32, _gla_gate_ep(1, 2)))
            o = _gla(q, k, la, v, g, a_out_norm[l], bsz, seq, n_chunk=tiles["gla_chunks"])
        else:
            jl = l - n_a
            q, g, mo = _norm_proj(
                x2, mix_norm[l], w_b, jl,
                [(main_w // tn, tn, BF16, _head_norm_ep(0, attn_scale * LOG2E)),
                 (main_w // tn, tn, BF16, _plain_ep),
                 (1, tn, BF16, _mem_attn_ep(1, 2, 3))],
                [b_q_norm[jl].reshape(1, HEAD_DIM), mem_q_norm[l].reshape(1, HEAD_DIM)],
                tm=tiles["proj_tm"], tn=tn, name="fox_in_proj", row_extras=mem_rows)
            o = _fox(q.reshape(bsz, seq, main_w), k_sh.reshape(bsz, seq, main_w),
                     v_sh.reshape(bsz, seq, main_w), k_decay, c2,
                     g.reshape(bsz, seq, main_w), t=tiles["fox_t"], rq=tiles["fox_rq"], hp=tiles["fox_hp"])
            o = o.reshape(n, main_w)

        x2 = _out_proj(x2, o, mo, w_out_b, l, tm=tiles["out_tm"])
        x2 = ffn(x2, l, 1)
    return x2.reshape(bsz, seq, d)


_TILES = dict(ffn_tm=1024, ffn_tf=512, proj_tm=1024, proj_tn=512, mem_tm=512, cum_ts=512,
              gla_chunks=16, fox_t=1024, fox_rq=256, fox_hp=4, out_tm=512)


def kernel(x, mem, ffn_norm, ffn_w1, ffn_w3, ffn_w2, mix_norm, mem_norm, w_mem_kv, mem_q_norm, mem_k_norm, w_out, a_w_in, a_w_gate_up, a_b_gate, a_out_norm, b_w_in, b_q_norm, kv_norm, w_kv, b_f, k_norm):
    return _forward(x, mem, ffn_norm, ffn_w1, ffn_w3, ffn_w2, mix_norm, mem_norm, w_mem_kv,
                    mem_q_norm, mem_k_norm, w_out, a_w_in, a_w_gate_up, a_b_gate, a_out_norm,
                    b_w_in, b_q_norm, kv_norm, w_kv, b_f, k_norm, tiles=_TILES)
```

```python
import functools

import numpy as np
import jax
import jax.numpy as jnp
from jax import lax
from jax.experimental import pallas as pl
from jax.experimental.pallas import tpu as pltpu

F32 = jnp.float32
BF16 = jnp.bfloat16

CHUNK = 64
HEAD_DIM = 128
MEM_HEADS = 4
MEM_W = MEM_HEADS * HEAD_DIM
GLA_HEADS = 4
GLA_GATE_RANK = 16
GLA_GATE_NORMALIZER = 16.0
EPS = 1e-6
LOG2E = 1.4426950408889634
DECAY_TERMS = 3

LANES = 128
V7X_VMEM_LIMIT_BYTES = 60 * 1024 * 1024

GLA_HK_PAD = 256


def _rms(x, gain):
    return x * lax.rsqrt(jnp.mean(x * x, axis=-1, keepdims=True) + EPS) * gain


def _log_sigmoid(x):
    return jnp.minimum(x, 0.0) - jnp.log1p(jnp.exp(-jnp.abs(x)))


def _silu(x):
    return x * jax.nn.sigmoid(x)


def _split3(x):
    hi = x.astype(BF16)
    r1 = x - hi.astype(F32)
    mid = r1.astype(BF16)
    lo = (r1 - mid.astype(F32)).astype(BF16)
    return hi, mid, lo


def _cumsum_rows(x):
    n = x.shape[0]
    row = lax.broadcasted_iota(jnp.int32, (n, n), 0)
    col = lax.broadcasted_iota(jnp.int32, (n, n), 1)
    tri = (row >= col).astype(BF16)
    return sum(jnp.dot(tri, term, preferred_element_type=F32) for term in _split3(x))


def _params(*semantics):
    return pltpu.CompilerParams(dimension_semantics=semantics,
                                vmem_limit_bytes=V7X_VMEM_LIMIT_BYTES)


def _ffn_body(x_ref, g_ref, w1_ref, w3_ref, w2_ref, o_ref, h_scr):
    def ff_tile(h):
        a = jnp.dot(h, w1_ref[...], preferred_element_type=F32)
        b = jnp.dot(h, w3_ref[...], preferred_element_type=F32)
        act = (_silu(a) * b * 0.5).astype(BF16)
        return jnp.dot(act, w2_ref[...].astype(BF16), preferred_element_type=F32)

    @pl.when(pl.program_id(1) == 0)
    def _():
        x = x_ref[...]
        h = _rms(x, g_ref[...]).astype(BF16)
        h_scr[...] = h
        o_ref[...] = x + ff_tile(h)

    @pl.when(pl.program_id(1) > 0)
    def _():
        o_ref[...] += ff_tile(h_scr[...])


def _ffn(x, gain, w1, w3, w2, layer, half, *, tm, tf):
    n, d = x.shape
    f = w1.shape[-1]
    return pl.pallas_call(
        _ffn_body,
        grid=(n // tm, f // tf),
        in_specs=[
            pl.BlockSpec((tm, d), lambda i, j: (i, 0)),
            pl.BlockSpec((1, d), lambda i, j: (0, 0)),
            pl.BlockSpec((None, None, d, tf), lambda i, j: (layer, half, 0, j)),
            pl.BlockSpec((None, None, d, tf), lambda i, j: (layer, half, 0, j)),
            pl.BlockSpec((None, None, tf, d), lambda i, j: (layer, half, j, 0)),
        ],
        out_specs=pl.BlockSpec((tm, d), lambda i, j: (i, 0)),
        out_shape=jax.ShapeDtypeStruct((n, d), F32),
        scratch_shapes=[pltpu.VMEM((tm, d), BF16)],
        compiler_params=_params("parallel", "arbitrary"),
        name="ffn",
    )(x, gain.reshape(1, d), w1, w3, w2)


def _norm_proj_body(*refs, groups, n_extra, lowrank_fn):
    x_ref, g_ref, w_ref = refs[:3]
    n_in = 3 + (lowrank_fn is not None)
    extra = refs[n_in:n_in + n_extra]
    outs = refs[n_in + n_extra:n_in + n_extra + len(groups)]
    if lowrank_fn is None:
        h_scr = refs[-1]
    else:
        w_lr_ref, lr_out_ref, h_scr, lr_scr = refs[3], refs[-3], refs[-2], refs[-1]
    j = pl.program_id(1)

    def tile(h, fn, o_ref):
        acc = jnp.dot(h, w_ref[...], preferred_element_type=F32)
        o_ref[...] = fn(acc, extra).astype(o_ref.dtype)

    @pl.when(j == 0)
    def _():
        h = _rms(x_ref[...], g_ref[...]).astype(BF16)
        h_scr[...] = h
        if lowrank_fn is not None:
            lr_scr[...] = jnp.dot(h, w_lr_ref[...], preferred_element_type=F32).astype(BF16)
        tile(h, groups[0][3], outs[0])

    start = 0
    for (n_tiles, _, _, fn), o_ref in zip(groups, outs):
        @pl.when((j >= max(start, 1)) & (j < start + n_tiles))
        def _(fn=fn, o_ref=o_ref):
            tile(h_scr[...], fn, o_ref)
        start += n_tiles

    if lowrank_fn is not None:
        @pl.when(j == start)
        def _():
            lr_out_ref[...] = lowrank_fn(lr_scr[...], extra).astype(lr_out_ref.dtype)


def _norm_proj(x, gain, w, layer, groups, extras, *, tm, tn, name, row_extras=(), lowrank=None):
    n, d = x.shape
    n_w_tiles = sum(g[0] for g in groups)
    assert w.shape[2] == tn * n_w_tiles
    in_specs = [
        pl.BlockSpec((tm, d), lambda i, j: (i, 0)),
        pl.BlockSpec((1, d), lambda i, j: (0, 0)),
        pl.BlockSpec((None, d, tn), lambda i, j: (layer, 0, jnp.minimum(j, n_w_tiles - 1))),
    ]
    operands = [x, gain.reshape(1, d), w]
    scratch = [pltpu.VMEM((tm, d), BF16)]
    if lowrank is not None:
        in_specs.append(pl.BlockSpec(lowrank[0].shape, lambda i, j: (0, 0)))
        operands.append(lowrank[0])
        scratch.append(pltpu.VMEM((tm, lowrank[0].shape[1]), BF16))
    in_specs += [pl.BlockSpec(e.shape, lambda i, j: (0, 0)) for e in extras] + [
        pl.BlockSpec((rows, a.shape[1]), lambda i, j, per=per: (i // per, 0))
        for a, rows, per in row_extras]
    operands += list(extras) + [a for a, _, _ in row_extras]
    out_specs, out_shapes = [], []
    start = 0
    for n_tiles, width, dtype, _ in groups:
        out_specs.append(pl.BlockSpec(
            (tm, width),
            lambda i, j, s=start, c=n_tiles: (i, jnp.clip(j - s, 0, c - 1))))
        out_shapes.append(jax.ShapeDtypeStruct((n, width * n_tiles), dtype))
        start += n_tiles
    if lowrank is not None:
        out_specs.append(pl.BlockSpec((tm, lowrank[1]), lambda i, j: (i, 0)))
        out_shapes.append(jax.ShapeDtypeStruct((n, lowrank[1]), lowrank[2]))
        start += 1
    return pl.pallas_call(
        functools.partial(_norm_proj_body, groups=groups,
                          n_extra=len(extras) + len(row_extras),
                          lowrank_fn=None if lowrank is None else lowrank[3]),
        grid=(n // tm, start),
        in_specs=in_specs,
        out_specs=out_specs,
        out_shape=out_shapes,
        scratch_shapes=scratch,
        compiler_params=_params("parallel", "arbitrary"),
        name=name,
    )(*operands)


def _scale_ep(scale):
    return lambda acc, extra: acc * scale


def _plain_ep(acc, extra):
    return acc


def _head_norm_ep(gain_idx, scale):
    def fn(acc, extra):
        gain = extra[gain_idx][...]
        parts = []
        for h in range(acc.shape[1] // HEAD_DIM):
            a = acc[:, h * HEAD_DIM:(h + 1) * HEAD_DIM]
            parts.append(_rms(a, gain) * scale)
        return jnp.concatenate(parts, axis=-1)
    return fn


def _mem_attn_ep(gain_idx, mk_idx, mv_idx):
    norm = _head_norm_ep(gain_idx, HEAD_DIM ** -0.5)

    def fn(acc, extra):
        qm = norm(acc, extra).astype(BF16)
        mk_ref, mv_ref = extra[mk_idx], extra[mv_idx]
        parts = []
        for h in range(MEM_HEADS):
            hs = slice(h * HEAD_DIM, (h + 1) * HEAD_DIM)
            s = lax.dot_general(qm[:, hs], mk_ref[:, hs], (((1,), (1,)), ((), ())),
                                preferred_element_type=F32)
            e = jnp.exp(s - jnp.max(s, axis=1, keepdims=True))
            p = e / jnp.sum(e, axis=1, keepdims=True)
            parts.append(jnp.dot(p.astype(BF16), mv_ref[:, hs], preferred_element_type=F32))
        return jnp.concatenate(parts, axis=-1)
    return fn


def _gla_gate_ep(w_up_idx, bias_idx):
    def fn(lr, extra):
        z = jnp.dot(lr, extra[w_up_idx][...], preferred_element_type=F32)
        return _log_sigmoid(z + extra[bias_idx][...]) / GLA_GATE_NORMALIZER
    return fn


def _forget_gate_ep(bias_idx):
    def fn(acc, extra):
        return _log_sigmoid(acc[:, :LANES] + extra[bias_idx][...])
    return fn


def _cumsum_body(lf_ref, col_ref, kd_ref, carry):
    @pl.when(pl.program_id(1) == 0)
    def _():
        carry[...] = jnp.zeros_like(carry)

    cum = _cumsum_rows(lf_ref[...]) + carry[...]
    carry[...] = cum[cum.shape[0] - 1:, :]
    c2 = cum * LOG2E
    col_ref[...] = c2
    terms = [term.astype(F32) for term in _split3(-c2)]
    lane = lax.broadcasted_iota(jnp.int32, c2.shape, 1)
    for h in range(kd_ref.shape[0]):
        tile = jnp.zeros_like(c2)
        for i, term in enumerate(terms):
            tile = jnp.where(lane == i, term[:, h:h + 1], tile)
        kd_ref[h] = tile.astype(BF16)


def _seq_cumsum(logf, bsz, seq, heads, *, ts):
    return pl.pallas_call(
        _cumsum_body,
        grid=(bsz, seq // ts),
        in_specs=[pl.BlockSpec((None, ts, LANES), lambda b, s: (b, s, 0))],
        out_specs=[pl.BlockSpec((None, ts, LANES), lambda b, s: (b, s, 0)),
                   pl.BlockSpec((None, heads, ts, LANES), lambda b, s: (b, 0, s, 0))],
        out_shape=[jax.ShapeDtypeStruct((bsz, seq, LANES), F32),
                   jax.ShapeDtypeStruct((bsz, heads, seq, LANES), BF16)],
        scratch_shapes=[pltpu.VMEM((1, LANES), F32)],
        compiler_params=_params("parallel", "arbitrary"),
        name="forget_cumsum",
    )(logf.reshape(bsz, seq, LANES))


def _gla_body(q_ref, k_ref, la_ref, v_ref, g_ref, gain_ref, o_ref, st_ref, *, n_chunk, hv):
    @pl.when(pl.program_id(1) == 0)
    def _():
        st_ref[...] = jnp.zeros_like(st_ref)

    gain = gain_ref[...]
    for c in range(n_chunk):
        rows = pl.ds(c * CHUNK, CHUNK)
        cum = _cumsum_rows(la_ref[rows, :])
        total = cum[CHUNK - 1:, :]
        k_dec = (k_ref[rows, :] * jnp.exp(total - cum)).astype(BF16)
        a_chunk = jnp.exp(total)
        for h in range(GLA_HEADS):
            ks = slice(h * GLA_HK_PAD, (h + 1) * GLA_HK_PAD)
            vs = slice(h * hv, (h + 1) * hv)
            upd = lax.dot_general(v_ref[rows, vs], k_dec[:, ks],
                                  (((0,), (0,)), ((), ())), preferred_element_type=F32)
            st = a_chunk[:, ks] * st_ref[h] + upd
            st_ref[h] = st
            o = lax.dot_general(q_ref[rows, ks], st.astype(BF16),
                                (((1,), (1,)), ((), ())), preferred_element_type=F32)
            o = _rms(o, gain) * _silu(g_ref[rows, vs].astype(F32))
            o_ref[rows, vs] = o.astype(BF16)


def _gla(q, k, la, v, g, gain, bsz, seq, *, n_chunk):
    hv = v.shape[1] // GLA_HEADS
    kw = q.shape[1]
    tc = n_chunk * CHUNK
    nt = seq // tc

    def spec(width):
        return pl.BlockSpec((tc, width), lambda b, c: (b * nt + c, 0))

    return pl.pallas_call(
        functools.partial(_gla_body, n_chunk=n_chunk, hv=hv),
        grid=(bsz, nt),
        in_specs=[spec(kw), spec(kw), spec(kw), spec(v.shape[1]), spec(v.shape[1]),
                  pl.BlockSpec((1, hv), lambda b, c: (0, 0))],
        out_specs=spec(v.shape[1]),
        out_shape=jax.ShapeDtypeStruct(v.shape, BF16),
        scratch_shapes=[pltpu.VMEM((GLA_HEADS, hv, GLA_HK_PAD), F32)],
        compiler_params=_params("parallel", "arbitrary"),
        name="gla",
    )(q, k, la, v, g, gain.reshape(1, hv))


def _fox_body(qi_tab, ki_tab, q_ref, k_ref, v_ref, vprev_ref, kd_ref, ccol_ref, g_ref, o_ref,
              qa_scr, ka_scr, va_scr, vpa_scr, p_scr, m_scr, accl_scr, cc_scr, *, t, rq, hp):
    step = pl.program_id(2)
    head0 = pl.program_id(1) * hp
    qi = qi_tab[step]
    ki = ki_tab[step]

    def start_query_tile():
        m_scr[...] = jnp.full_like(m_scr, -jnp.inf)
        accl_scr[...] = jnp.zeros_like(accl_scr)
        blk = ccol_ref[...]
        lane = lax.broadcasted_iota(jnp.int32, blk.shape, 1)
        for j in range(hp):
            va_scr[j, :, HEAD_DIM:] = (lane == 0).astype(BF16)
            vpa_scr[j, :, HEAD_DIM:] = (lane == 0).astype(BF16)
            cc_scr[j] = jnp.sum(jnp.where(lane == head0 + j, blk, 0.0), axis=1, keepdims=True)
            qa_scr[j, :, :HEAD_DIM] = q_ref[:, j * HEAD_DIM:(j + 1) * HEAD_DIM]
            qa_scr[j, :, HEAD_DIM:] = (lane < DECAY_TERMS).astype(BF16)

    def load_keys(pending):
        for j in range(hp):
            hs = slice(j * HEAD_DIM, (j + 1) * HEAD_DIM)
            ka_scr[j, :, :HEAD_DIM] = k_ref[:, hs]
            ka_scr[j, :, HEAD_DIM:] = kd_ref[j]
            if pending:
                vpa_scr[j, :, :HEAD_DIM] = vprev_ref[:, hs]

    def update(j, r, n_cols, masked, pending):
        rows = pl.ds(r * rq, rq)
        u = lax.dot_general(qa_scr[j, rows, :], ka_scr[j, :n_cols, :], (((1,), (1,)), ((), ())),
                            preferred_element_type=F32)
        if masked:
            tail = u[:, n_cols - rq:]
            row = lax.broadcasted_iota(jnp.int32, tail.shape, 0)
            col = lax.broadcasted_iota(jnp.int32, tail.shape, 1)
            tail = jnp.where(col <= row, tail, -jnp.inf)
            u = tail if n_cols == rq else jnp.concatenate([u[:, :n_cols - rq], tail], axis=1)
        cc = cc_scr[j, rows, :]
        m_prev = m_scr[j, rows, :]
        m_new = jnp.maximum(m_prev, jnp.max(u, axis=1, keepdims=True) + cc)
        p = jnp.exp2(u + (cc - m_new))
        alpha = jnp.exp2(m_prev - m_new)
        m_scr[j, rows, :] = m_new
        accl = accl_scr[j, rows, :]
        if pending:
            accl = accl + jnp.dot(p_scr[j, rows, :], vpa_scr[j], preferred_element_type=F32)
        accl_scr[j, rows, :] = alpha * accl
        p_scr[j, rows, :n_cols] = p.astype(BF16)

    n_sub = t // rq

    def tile(masked, pending):
        for r in range(n_sub):
            for j in range(hp):
                update(j, r, (r + 1) * rq if masked else t, masked, pending)

    for diag in (False, True):
        for pending in (False, True):
            first = ki == 0

            @pl.when(((ki == qi) if diag else (ki < qi)) & (~first if pending else first))
            def _(diag=diag, pending=pending):
                if not pending:
                    start_query_tile()
                load_keys(pending)
                tile(diag, pending)

    @pl.when(ki == qi)
    def _():
        for j in range(hp):
            hs = slice(j * HEAD_DIM, (j + 1) * HEAD_DIM)
            va_scr[j, :, :HEAD_DIM] = v_ref[:, hs]
            for r in range(n_sub):
                rows = pl.ds(r * rq, rq)
                n_cols = (r + 1) * rq
                accl_scr[j, rows, :] += jnp.dot(p_scr[j, rows, :n_cols], va_scr[j, :n_cols, :],
                                                preferred_element_type=F32)
            accl = accl_scr[j]
            o = accl[:, :HEAD_DIM] / accl[:, HEAD_DIM:HEAD_DIM + 1]
            o_ref[:, hs] = (o * jax.nn.sigmoid(g_ref[:, hs].astype(F32))).astype(BF16)


def _fox(q, k, v, kd, c2, g, *, t, rq, hp):
    bsz, seq, width = q.shape
    heads = width // HEAD_DIM
    hw = hp * HEAD_DIM
    pairs = [(qi, ki) for qi in range(seq // t) for ki in range(qi + 1)]
    qi_tab = jnp.asarray(np.array([p[0] for p in pairs], np.int32))
    ki_tab = jnp.asarray(np.array([p[1] for p in pairs], np.int32))
    q_spec = pl.BlockSpec((None, t, hw), lambda b, h, s, qt, kt: (b, qt[s], h))
    kv_spec = pl.BlockSpec((None, t, hw), lambda b, h, s, qt, kt: (b, kt[s], h))
    vprev_spec = pl.BlockSpec((None, t, hw),
                              lambda b, h, s, qt, kt: (b, jnp.maximum(kt[s] - 1, 0), h))
    grid_spec = pltpu.PrefetchScalarGridSpec(
        num_scalar_prefetch=2,
        grid=(bsz, heads // hp, len(pairs)),
        in_specs=[
            q_spec, kv_spec, kv_spec, vprev_spec,
            pl.BlockSpec((None, hp, t, LANES), lambda b, h, s, qt, kt: (b, h, kt[s], 0)),
            pl.BlockSpec((None, t, LANES), lambda b, h, s, qt, kt: (b, qt[s], 0)),
            q_spec,
        ],
        out_specs=q_spec,
        scratch_shapes=[pltpu.VMEM((hp, t, 2 * HEAD_DIM), BF16),
                        pltpu.VMEM((hp, t, 2 * HEAD_DIM), BF16),
                        pltpu.VMEM((hp, t, 2 * HEAD_DIM), BF16),
                        pltpu.VMEM((hp, t, 2 * HEAD_DIM), BF16),
                        pltpu.VMEM((hp, t, t), BF16),
                        pltpu.VMEM((hp, t, 1), F32),
                        pltpu.VMEM((hp, t, 2 * HEAD_DIM), F32), pltpu.VMEM((hp, t, 1), F32)],
    )
    return pl.pallas_call(
        functools.partial(_fox_body, t=t, rq=rq, hp=hp),
        grid_spec=grid_spec,
        out_shape=jax.ShapeDtypeStruct(q.shape, BF16),
        compiler_params=_params("parallel", "parallel", "arbitrary"),
        name="fox",
    )(qi_tab, ki_tab, q, k, v, v, kd, c2, g)


def _out_proj_body(x_ref, o_ref, mo_ref, wo_ref, wm_ref, y_ref):
    y = x_ref[...] + jnp.dot(o_ref[...], wo_ref[...], preferred_element_type=F32)
    y_ref[...] = y + jnp.dot(mo_ref[...], wm_ref[...], preferred_element_type=F32)


def _out_proj(x, o, mo, w_out, layer, *, tm):
    n, d = x.shape
    main_w = o.shape[1]
    assert main_w % MEM_W == 0 and mo.shape[1] == MEM_W
    return pl.pallas_call(
        _out_proj_body,
        grid=(n // tm,),
        in_specs=[pl.BlockSpec((tm, d), lambda i: (i, 0)),
                  pl.BlockSpec((tm, main_w), lambda i: (i, 0)),
                  pl.BlockSpec((tm, MEM_W), lambda i: (i, 0)),
                  pl.BlockSpec((None, main_w, d), lambda i: (layer, 0, 0)),
                  pl.BlockSpec((None, MEM_W, d), lambda i: (layer, main_w // MEM_W, 0))],
        out_specs=pl.BlockSpec((tm, d), lambda i: (i, 0)),
        out_shape=jax.ShapeDtypeStruct((n, d), F32),
        compiler_params=_params("parallel"),
        name="out_proj",
    )(x, o, mo, w_out, w_out)


def _pad_cols(w, width):
    return jnp.pad(w, [(0, 0)] * (w.ndim - 1) + [(0, width - w.shape[-1])])


def _pad_gla_heads(w, hk):
    lead = w.shape[:-1]
    w = w.reshape(lead + (GLA_HEADS, hk))
    w = jnp.pad(w, [(0, 0)] * len(lead) + [(0, 0), (0, GLA_HK_PAD - hk)])
    return w.reshape(lead + (GLA_HEADS * GLA_HK_PAD,))


def _forward(x, mem, ffn_norm, ffn_w1, ffn_w3, ffn_w2, mix_norm, mem_norm, w_mem_kv,
             mem_q_norm, mem_k_norm, w_out, a_w_in, a_w_gate_up, a_b_gate, a_out_norm,
             b_w_in, b_q_norm, kv_norm, w_kv, b_f, k_norm, *, tiles):
    bsz, seq, d = x.shape
    n = bsz * seq
    depth = ffn_norm.shape[0]
    n_a = a_w_in.shape[0]
    main_w = d - MEM_W
    gla_key = main_w // 2
    hk = gla_key // GLA_HEADS
    fox_heads = main_w // HEAD_DIM
    tn = tiles["proj_tn"]
    attn_scale = HEAD_DIM ** -0.5

    w1 = ffn_w1.astype(BF16)
    w3 = ffn_w3.astype(BF16)
    w2 = ffn_w2
    w_out_b = w_out.astype(BF16)
    w_mem_b = w_mem_kv.astype(BF16)
    w_b = b_w_in.astype(BF16)
    w_kv_b = jnp.concatenate([w_kv[:, :2 * main_w], _pad_cols(w_kv[:, 2 * main_w:], tn)],
                             axis=1).astype(BF16)[None]
    c = np.cumsum([0, gla_key, gla_key, main_w, GLA_GATE_RANK, main_w, MEM_W])
    kw = GLA_HEADS * GLA_HK_PAD
    w_a = jnp.concatenate([
        _pad_gla_heads(a_w_in[..., c[0]:c[1]], hk),
        _pad_gla_heads(a_w_in[..., c[1]:c[2]], hk),
        a_w_in[..., c[2]:c[3]],
        a_w_in[..., c[4]:c[5]],
        a_w_in[..., c[5]:c[6]],
    ], axis=-1).astype(BF16)
    w_lr = _pad_cols(a_w_in[..., c[3]:c[4]], LANES).astype(BF16)
    w_up = jnp.pad(_pad_gla_heads(a_w_gate_up, hk),
                   ((0, 0), (0, LANES - GLA_GATE_RANK), (0, 0))).astype(BF16)
    b_gate = _pad_gla_heads(a_b_gate, hk)[:, None, :]
    x2 = x.reshape(n, d)
    mem2 = mem.reshape(bsz * mem.shape[1], d)

    def ffn(x2, l, s):
        return _ffn(x2, ffn_norm[l, s], w1, w3, w2, l, s,
                    tm=tiles["ffn_tm"], tf=tiles["ffn_tf"])

    k_sh = v_sh = c2 = k_decay = None
    for l in range(depth):
        if l == n_a:
            bias = _pad_cols(b_f.reshape(1, fox_heads), LANES)
            k_sh, v_sh, logf = _norm_proj(
                x2, kv_norm, w_kv_b, 0,
                [(main_w // tn, tn, BF16, _head_norm_ep(0, 1.0)),
                 (main_w // tn, tn, BF16, _plain_ep),
                 (1, LANES, F32, _forget_gate_ep(1))],
                [k_norm.reshape(1, HEAD_DIM), bias],
                tm=tiles["proj_tm"], tn=tn, name="kv_proj")
            c2, k_decay = _seq_cumsum(logf, bsz, seq, fox_heads, ts=tiles["cum_ts"])

        x2 = ffn(x2, l, 0)

        mk, mv = _norm_proj(
            mem2, mem_norm[l], w_mem_b, l,
            [(MEM_W // tn, tn, BF16, _head_norm_ep(0, 1.0)),
             (MEM_W // tn, tn, BF16, _plain_ep)],
            [mem_k_norm[l].reshape(1, HEAD_DIM)],
            tm=tiles["mem_tm"], tn=tn, name="mem_kv_proj")
        assert MEM_W == tn, "the memory-attention epilogue needs all memory heads in one tile"
        mem_rows = [(mk, mem.shape[1], seq // tiles["proj_tm"]),
                    (mv, mem.shape[1], seq // tiles["proj_tm"])]

        if l < n_a:
            q, k, v, g, mo, la = _norm_proj(
                x2, mix_norm[l], w_a, l,
                [(kw // tn, tn, BF16, _scale_ep(hk ** -0.5)),
                 (kw // tn, tn, F32, _plain_ep),
                 (main_w // tn, tn, BF16, _plain_ep),
                 (main_w // tn, tn, BF16, _plain_ep),
                 (1, tn, BF16, _mem_attn_ep(0, 3, 4))],
                [mem_q_norm[l].reshape(1, HEAD_DIM), w_up[l], b_gate[l]],
                tm=tiles["proj_tm"], tn=tn, name="gla_in_proj", row_extras=mem_rows,
                lowrank=(w_lr[l], kw, F32, _gla_gate_ep(1, 2)))
            o = _gla(q, k, la, v, g, a_out_norm[l], bsz, seq, n_chunk=tiles["gla_chunks"])
        else:
            jl = l - n_a
            q, g, mo = _norm_proj(
                x2, mix_norm[l], w_b, jl,
                [(main_w // tn, tn, BF16, _head_norm_ep(0, attn_scale * LOG2E)),
                 (main_w // tn, tn, BF16, _plain_ep),
                 (1, tn, BF16, _mem_attn_ep(1, 2, 3))],
                [b_q_norm[jl].reshape(1, HEAD_DIM), mem_q_norm[l].reshape(1, HEAD_DIM)],
                tm=tiles["proj_tm"], tn=tn, name="fox_in_proj", row_extras=mem_rows)
            o = _fox(q.reshape(bsz, seq, main_w), k_sh.reshape(bsz, seq, main_w),
                     v_sh.reshape(bsz, seq, main_w), k_decay, c2,
                     g.reshape(bsz, seq, main_w), t=tiles["fox_t"], rq=tiles["fox_rq"], hp=tiles["fox_hp"])
            o = o.reshape(n, main_w)

        x2 = _out_proj(x2, o, mo, w_out_b, l, tm=tiles["out_tm"])
        x2 = ffn(x2, l, 1)
    return x2.reshape(bsz, seq, d)


_TILES = dict(ffn_tm=1024, ffn_tf=512, proj_tm=1024, proj_tn=512, mem_tm=512, cum_ts=512,
              gla_chunks=16, fox_t=1024, fox_rq=256, fox_hp=4, out_tm=512)


def kernel(x, mem, ffn_norm, ffn_w1, ffn_w3, ffn_w2, mix_norm, mem_norm, w_mem_kv, mem_q_norm, mem_k_norm, w_out, a_w_in, a_w_gate_up, a_b_gate, a_out_norm, b_w_in, b_q_norm, kv_norm, w_kv, b_f, k_norm):
    return _forward(x, mem, ffn_norm, ffn_w1, ffn_w3, ffn_w2, mix_norm, mem_norm, w_mem_kv,
                    mem_q_norm, mem_k_norm, w_out, a_w_in, a_w_gate_up, a_b_gate, a_out_norm,
                    b_w_in, b_q_norm, kv_norm, w_kv, b_f, k_norm, tiles=_TILES)
```

```python
import functools

import numpy as np
import jax
import jax.numpy as jnp
from jax import lax
from jax.experimental import pallas as pl
from jax.experimental.pallas import tpu as pltpu

F32 = jnp.float32
BF16 = jnp.bfloat16

CHUNK = 64
HEAD_DIM = 128
MEM_HEADS = 4
MEM_W = MEM_HEADS * HEAD_DIM
GLA_HEADS = 4
GLA_GATE_RANK = 16
GLA_GATE_NORMALIZER = 16.0
EPS = 1e-6
LOG2E = 1.4426950408889634
DECAY_TERMS = 3

LANES = 128
V7X_VMEM_LIMIT_BYTES = 60 * 1024 * 1024

GLA_HK_PAD = 256


def _rms(x, gain):
    return x * lax.rsqrt(jnp.mean(x * x, axis=-1, keepdims=True) + EPS) * gain


def _log_sigmoid(x):
    return jnp.minimum(x, 0.0) - jnp.log1p(jnp.exp(-jnp.abs(x)))


def _silu(x):
    return x * jax.nn.sigmoid(x)


def _split3(x):
    hi = x.astype(BF16)
    r1 = x - hi.astype(F32)
    mid = r1.astype(BF16)
    lo = (r1 - mid.astype(F32)).astype(BF16)
    return hi, mid, lo


def _cumsum_rows(x):
    n = x.shape[0]
    row = lax.broadcasted_iota(jnp.int32, (n, n), 0)
    col = lax.broadcasted_iota(jnp.int32, (n, n), 1)
    tri = (row >= col).astype(BF16)
    return sum(jnp.dot(tri, term, preferred_element_type=F32) for term in _split3(x))


def _params(*semantics):
    return pltpu.CompilerParams(dimension_semantics=semantics,
                                vmem_limit_bytes=V7X_VMEM_LIMIT_BYTES)


def _ffn_body(x_ref, g_ref, w1_ref, w3_ref, w2_ref, o_ref, h_scr):
    def ff_tile(h):
        a = jnp.dot(h, w1_ref[...], preferred_element_type=F32)
        b = jnp.dot(h, w3_ref[...], preferred_element_type=F32)
        act = (_silu(a) * b * 0.5).astype(BF16)
        return jnp.dot(act, w2_ref[...].astype(BF16), preferred_element_type=F32)

    @pl.when(pl.program_id(1) == 0)
    def _():
        x = x_ref[...]
        h = _rms(x, g_ref[...]).astype(BF16)
        h_scr[...] = h
        o_ref[...] = x + ff_tile(h)

    @pl.when(pl.program_id(1) > 0)
    def _():
        o_ref[...] += ff_tile(h_scr[...])


def _ffn(x, gain, w1, w3, w2, layer, half, *, tm, tf):
    n, d = x.shape
    f = w1.shape[-1]
    return pl.pallas_call(
        _ffn_body,
        grid=(n // tm, f // tf),
        in_specs=[
            pl.BlockSpec((tm, d), lambda i, j: (i, 0)),
            pl.BlockSpec((1, d), lambda i, j: (0, 0)),
            pl.BlockSpec((None, None, d, tf), lambda i, j: (layer, half, 0, j)),
            pl.BlockSpec((None, None, d, tf), lambda i, j: (layer, half, 0, j)),
            pl.BlockSpec((None, None, tf, d), lambda i, j: (layer, half, j, 0)),
        ],
        out_specs=pl.BlockSpec((tm, d), lambda i, j: (i, 0)),
        out_shape=jax.ShapeDtypeStruct((n, d), F32),
        scratch_shapes=[pltpu.VMEM((tm, d), BF16)],
        compiler_params=_params("parallel", "arbitrary"),
        name="ffn",
    )(x, gain.reshape(1, d), w1, w3, w2)


def _norm_proj_body(*refs, groups, n_extra, lowrank_fn):
    x_ref, g_ref, w_ref = refs[:3]
    n_in = 3 + (lowrank_fn is not None)
    extra = refs[n_in:n_in + n_extra]
    outs = refs[n_in + n_extra:n_in + n_extra + len(groups)]
    if lowrank_fn is None:
        h_scr = refs[-1]
    else:
        w_lr_ref, lr_out_ref, h_scr, lr_scr = refs[3], refs[-3], refs[-2], refs[-1]
    j = pl.program_id(1)

    def tile(h, fn, o_ref):
        acc = jnp.dot(h, w_ref[...], preferred_element_type=F32)
        o_ref[...] = fn(acc, extra).astype(o_ref.dtype)

    @pl.when(j == 0)
    def _():
        h = _rms(x_ref[...], g_ref[...]).astype(BF16)
        h_scr[...] = h
        if lowrank_fn is not None:
            lr_scr[...] = jnp.dot(h, w_lr_ref[...], preferred_element_type=F32).astype(BF16)
        tile(h, groups[0][3], outs[0])

    start = 0
    for (n_tiles, _, _, fn), o_ref in zip(groups, outs):
        @pl.when((j >= max(start, 1)) & (j < start + n_tiles))
        def _(fn=fn, o_ref=o_ref):
            tile(h_scr[...], fn, o_ref)
        start += n_tiles

    if lowrank_fn is not None:
        @pl.when(j == start)
        def _():
            lr_out_ref[...] = lowrank_fn(lr_scr[...], extra).astype(lr_out_ref.dtype)


def _norm_proj(x, gain, w, layer, groups, extras, *, tm, tn, name, row_extras=(), lowrank=None):
    n, d = x.shape
    n_w_tiles = sum(g[0] for g in groups)
    assert w.shape[2] == tn * n_w_tiles
    in_specs = [
        pl.BlockSpec((tm, d), lambda i, j: (i, 0)),
        pl.BlockSpec((1, d), lambda i, j: (0, 0)),
        pl.BlockSpec((None, d, tn), lambda i, j: (layer, 0, jnp.minimum(j, n_w_tiles - 1))),
    ]
    operands = [x, gain.reshape(1, d), w]
    scratch = [pltpu.VMEM((tm, d), BF16)]
    if lowrank is not None:
        in_specs.append(pl.BlockSpec(lowrank[0].shape, lambda i, j: (0, 0)))
        operands.append(lowrank[0])
        scratch.append(pltpu.VMEM((tm, lowrank[0].shape[1]), BF16))
    in_specs += [pl.BlockSpec(e.shape, lambda i, j: (0, 0)) for e in extras] + [
        pl.BlockSpec((rows, a.shape[1]), lambda i, j, per=per: (i // per, 0))
        for a, rows, per in row_extras]
    operands += list(extras) + [a for a, _, _ in row_extras]
    out_specs, out_shapes = [], []
    start = 0
    for n_tiles, width, dtype, _ in groups:
        out_specs.append(pl.BlockSpec(
            (tm, width),
            lambda i, j, s=start, c=n_tiles: (i, jnp.clip(j - s, 0, c - 1))))
        out_shapes.append(jax.ShapeDtypeStruct((n, width * n_tiles), dtype))
        start += n_tiles
    if lowrank is not None:
        out_specs.append(pl.BlockSpec((tm, lowrank[1]), lambda i, j: (i, 0)))
        out_shapes.append(jax.ShapeDtypeStruct((n, lowrank[1]), lowrank[2]))
        start += 1
    return pl.pallas_call(
        functools.partial(_norm_proj_body, groups=groups,
                          n_extra=len(extras) + len(row_extras),
                          lowrank_fn=None if lowrank is None else lowrank[3]),
        grid=(n // tm, start),
        in_specs=in_specs,
        out_specs=out_specs,
        out_shape=out_shapes,
        scratch_shapes=scratch,
        compiler_params=_params("parallel", "arbitrary"),
        name=name,
    )(*operands)


def _scale_ep(scale):
    return lambda acc, extra: acc * scale


def _plain_ep(acc, extra):
    return acc


def _head_norm_ep(gain_idx, scale):
    def fn(acc, extra):
        gain = extra[gain_idx][...]
        parts = []
        for h in range(acc.shape[1] // HEAD_DIM):
            a = acc[:, h * HEAD_DIM:(h + 1) * HEAD_DIM]
            parts.append(_rms(a, gain) * scale)
        return jnp.concatenate(parts, axis=-1)
    return fn


def _mem_attn_ep(gain_idx, mk_idx, mv_idx):
    norm = _head_norm_ep(gain_idx, HEAD_DIM ** -0.5)

    def fn(acc, extra):
        qm = norm(acc, extra).astype(BF16)
        mk_ref, mv_ref = extra[mk_idx], extra[mv_idx]
        parts = []
        for h in range(MEM_HEADS):
            hs = slice(h * HEAD_DIM, (h + 1) * HEAD_DIM)
            s = lax.dot_general(qm[:, hs], mk_ref[:, hs], (((1,), (1,)), ((), ())),
                                preferred_element_type=F32)
            e = jnp.exp(s - jnp.max(s, axis=1, keepdims=True))
            p = e / jnp.sum(e, axis=1, keepdims=True)
            parts.append(jnp.dot(p.astype(BF16), mv_ref[:, hs], preferred_element_type=F32))
        return jnp.concatenate(parts, axis=-1)
    return fn


def _gla_gate_ep(w_up_idx, bias_idx):
    def fn(lr, extra):
        z = jnp.dot(lr, extra[w_up_idx][...], preferred_element_type=F32)
        return _log_sigmoid(z + extra[bias_idx][...]) / GLA_GATE_NORMALIZER
    return fn


def _forget_gate_ep(bias_idx):
    def fn(acc, extra):
        return _log_sigmoid(acc[:, :LANES] + extra[bias_idx][...])
    return fn


def _cumsum_body(lf_ref, col_ref, kd_ref, carry):
    @pl.when(pl.program_id(1) == 0)
    def _():
        carry[...] = jnp.zeros_like(carry)

    cum = _cumsum_rows(lf_ref[...]) + carry[...]
    carry[...] = cum[cum.shape[0] - 1:, :]
    c2 = cum * LOG2E
    col_ref[...] = c2
    terms = [term.astype(F32) for term in _split3(-c2)]
    lane = lax.broadcasted_iota(jnp.int32, c2.shape, 1)
    for h in range(kd_ref.shape[0]):
        tile = jnp.zeros_like(c2)
        for i, term in enumerate(terms):
            tile = jnp.where(lane == i, term[:, h:h + 1], tile)
        kd_ref[h] = tile.astype(BF16)


def _seq_cumsum(logf, bsz, seq, heads, *, ts):
    return pl.pallas_call(
        _cumsum_body,
        grid=(bsz, seq // ts),
        in_specs=[pl.BlockSpec((None, ts, LANES), lambda b, s: (b, s, 0))],
        out_specs=[pl.BlockSpec((None, ts, LANES), lambda b, s: (b, s, 0)),
                   pl.BlockSpec((None, heads, ts, LANES), lambda b, s: (b, 0, s, 0))],
        out_shape=[jax.ShapeDtypeStruct((bsz, seq, LANES), F32),
                   jax.ShapeDtypeStruct((bsz, heads, seq, LANES), BF16)],
        scratch_shapes=[pltpu.VMEM((1, LANES), F32)],
        compiler_params=_params("parallel", "arbitrary"),
        name="forget_cumsum",
    )(logf.reshape(bsz, seq, LANES))


def _gla_body(q_ref, k_ref, la_ref, v_ref, g_ref, gain_ref, o_ref, st_ref, *, n_chunk, hv):
    @pl.when(pl.program_id(1) == 0)
    def _():
        st_ref[...] = jnp.zeros_like(st_ref)

    gain = gain_ref[...]
    for c in range(n_chunk):
        rows = pl.ds(c * CHUNK, CHUNK)
        cum = _cumsum_rows(la_ref[rows, :])
        total = cum[CHUNK - 1:, :]
        k_dec = (k_ref[rows, :] * jnp.exp(total - cum)).astype(BF16)
        a_chunk = jnp.exp(total)
        for h in range(GLA_HEADS):
            ks = slice(h * GLA_HK_PAD, (h + 1) * GLA_HK_PAD)
            vs = slice(h * hv, (h + 1) * hv)
            upd = lax.dot_general(v_ref[rows, vs], k_dec[:, ks],
                                  (((0,), (0,)), ((), ())), preferred_element_type=F32)
            st = a_chunk[:, ks] * st_ref[h] + upd
            st_ref[h] = st
            o = lax.dot_general(q_ref[rows, ks], st.astype(BF16),
                                (((1,), (1,)), ((), ())), preferred_element_type=F32)
            o = _rms(o, gain) * _silu(g_ref[rows, vs].astype(F32))
            o_ref[rows, vs] = o.astype(BF16)


def _gla(q, k, la, v, g, gain, bsz, seq, *, n_chunk):
    hv = v.shape[1] // GLA_HEADS
    kw = q.shape[1]
    tc = n_chunk * CHUNK
    nt = seq // tc

    def spec(width):
        return pl.BlockSpec((tc, width), lambda b, c: (b * nt + c, 0))

    return pl.pallas_call(
        functools.partial(_gla_body, n_chunk=n_chunk, hv=hv),
        grid=(bsz, nt),
        in_specs=[spec(kw), spec(kw), spec(kw), spec(v.shape[1]), spec(v.shape[1]),
                  pl.BlockSpec((1, hv), lambda b, c: (0, 0))],
        out_specs=spec(v.shape[1]),
        out_shape=jax.ShapeDtypeStruct(v.shape, BF16),
        scratch_shapes=[pltpu.VMEM((GLA_HEADS, hv, GLA_HK_PAD), F32)],
        compiler_params=_params("parallel", "arbitrary"),
        name="gla",
    )(q, k, la, v, g, gain.reshape(1, hv))


def _fox_body(qi_tab, ki_tab, q_ref, k_ref, v_ref, vprev_ref, kd_ref, ccol_ref, g_ref, o_ref,
              qa_scr, ka_scr, va_scr, vpa_scr, p_scr, m_scr, accl_scr, cc_scr, *, t, rq, hp):
    step = pl.program_id(2)
    head0 = pl.program_id(1) * hp
    qi = qi_tab[step]
    ki = ki_tab[step]

    def start_query_tile():
        m_scr[...] = jnp.full_like(m_scr, -jnp.inf)
        accl_scr[...] = jnp.zeros_like(accl_scr)
        blk = ccol_ref[...]
        lane = lax.broadcasted_iota(jnp.int32, blk.shape, 1)
        for j in range(hp):
            va_scr[j, :, HEAD_DIM:] = (lane == 0).astype(BF16)
            vpa_scr[j, :, HEAD_DIM:] = (lane == 0).astype(BF16)
            cc_scr[j] = jnp.sum(jnp.where(lane == head0 + j, blk, 0.0), axis=1, keepdims=True)
            qa_scr[j, :, :HEAD_DIM] = q_ref[:, j * HEAD_DIM:(j + 1) * HEAD_DIM]
            qa_scr[j, :, HEAD_DIM:] = (lane < DECAY_TERMS).astype(BF16)

    def load_keys(pending):
        for j in range(hp):
            hs = slice(j * HEAD_DIM, (j + 1) * HEAD_DIM)
            ka_scr[j, :, :HEAD_DIM] = k_ref[:, hs]
            ka_scr[j, :, HEAD_DIM:] = kd_ref[j]
            if pending:
                vpa_scr[j, :, :HEAD_DIM] = vprev_ref[:, hs]

    def update(j, r, n_cols, masked, pending):
        rows = pl.ds(r * rq, rq)
        u = lax.dot_general(qa_scr[j, rows, :], ka_scr[j, :n_cols, :], (((1,), (1,)), ((), ())),
                            preferred_element_type=F32)
        if masked:
            tail = u[:, n_cols - rq:]
            row = lax.broadcasted_iota(jnp.int32, tail.shape, 0)
            col = lax.broadcasted_iota(jnp.int32, tail.shape, 1)
            tail = jnp.where(col <= row, tail, -jnp.inf)
            u = tail if n_cols == rq else jnp.concatenate([u[:, :n_cols - rq], tail], axis=1)
        cc = cc_scr[j, rows, :]
        m_prev = m_scr[j, rows, :]
        m_new = jnp.maximum(m_prev, jnp.max(u, axis=1, keepdims=True) + cc)
        p = jnp.exp2(u + (cc - m_new))
        alpha = jnp.exp2(m_prev - m_new)
        m_scr[j, rows, :] = m_new
        accl = accl_scr[j, rows, :]
        if pending:
            accl = accl + jnp.dot(p_scr[j, rows, :], vpa_scr[j], preferred_element_type=F32)
        accl_scr[j, rows, :] = alpha * accl
        p_scr[j, rows, :n_cols] = p.astype(BF16)

    n_sub = t // rq

    def tile(masked, pending):
        for r in range(n_sub):
            for j in range(hp):
                update(j, r, (r + 1) * rq if masked else t, masked, pending)

    for diag in (False, True):
        for pending in (False, True):
            first = ki == 0

            @pl.when(((ki == qi) if diag else (ki < qi)) & (~first if pending else first))
            def _(diag=diag, pending=pending):
                if not pending:
                    start_query_tile()
                load_keys(pending)
                tile(diag, pending)

    @pl.when(ki == qi)
    def _():
        for j in range(hp):
            hs = slice(j * HEAD_DIM, (j + 1) * HEAD_DIM)
            va_scr[j, :, :HEAD_DIM] = v_ref[:, hs]
            for r in range(n_sub):
                rows = pl.ds(r * rq, rq)
                n_cols = (r + 1) * rq
                accl_scr[j, rows, :] += jnp.dot(p_scr[j, rows, :n_cols], va_scr[j, :n_cols, :],
                                                preferred_element_type=F32)
            accl = accl_scr[j]
            o = accl[:, :HEAD_DIM] / accl[:, HEAD_DIM:HEAD_DIM + 1]
            o_ref[:, hs] = (o * jax.nn.sigmoid(g_ref[:, hs].astype(F32))).astype(BF16)


def _fox(q, k, v, kd, c2, g, *, t, rq, hp):
    bsz, seq, width = q.shape
    heads = width // HEAD_DIM
    hw = hp * HEAD_DIM
    pairs = [(qi, ki) for qi in range(seq // t) for ki in range(qi + 1)]
    qi_tab = jnp.asarray(np.array([p[0] for p in pairs], np.int32))
    ki_tab = jnp.asarray(np.array([p[1] for p in pairs], np.int32))
    q_spec = pl.BlockSpec((None, t, hw), lambda b, h, s, qt, kt: (b, qt[s], h))
    kv_spec = pl.BlockSpec((None, t, hw), lambda b, h, s, qt, kt: (b, kt[s], h))
    vprev_spec = pl.BlockSpec((None, t, hw),
                              lambda b, h, s, qt, kt: (b, jnp.maximum(kt[s] - 1, 0), h))
    grid_spec = pltpu.PrefetchScalarGridSpec(
        num_scalar_prefetch=2,
        grid=(bsz, heads // hp, len(pairs)),
        in_specs=[
            q_spec, kv_spec, kv_spec, vprev_spec,
            pl.BlockSpec((None, hp, t, LANES), lambda b, h, s, qt, kt: (b, h, kt[s], 0)),
            pl.BlockSpec((None, t, LANES), lambda b, h, s, qt, kt: (b, qt[s], 0)),
            q_spec,
        ],
        out_specs=q_spec,
        scratch_shapes=[pltpu.VMEM((hp, t, 2 * HEAD_DIM), BF16),
                        pltpu.VMEM((hp, t, 2 * HEAD_DIM), BF16),
                        pltpu.VMEM((hp, t, 2 * HEAD_DIM), BF16),
                        pltpu.VMEM((hp, t, 2 * HEAD_DIM), BF16),
                        pltpu.VMEM((hp, t, t), BF16),
                        pltpu.VMEM((hp, t, 1), F32),
                        pltpu.VMEM((hp, t, 2 * HEAD_DIM), F32), pltpu.VMEM((hp, t, 1), F32)],
    )
    return pl.pallas_call(
        functools.partial(_fox_body, t=t, rq=rq, hp=hp),
        grid_spec=grid_spec,
        out_shape=jax.ShapeDtypeStruct(q.shape, BF16),
        compiler_params=_params("parallel", "parallel", "arbitrary"),
        name="fox",
    )(qi_tab, ki_tab, q, k, v, v, kd, c2, g)


def _out_proj_body(x_ref, o_ref, mo_ref, wo_ref, wm_ref, y_ref):
    y = x_ref[...] + jnp.dot(o_ref[...], wo_ref[...], preferred_element_type=F32)
    y_ref[...] = y + jnp.dot(mo_ref[...], wm_ref[...], preferred_element_type=F32)


def _out_proj(x, o, mo, w_out, layer, *, tm):
    n, d = x.shape
    main_w = o.shape[1]
    assert main_w % MEM_W == 0 and mo.shape[1] == MEM_W
    return pl.pallas_call(
        _out_proj_body,
        grid=(n // tm,),
        in_specs=[pl.BlockSpec((tm, d), lambda i: (i, 0)),
                  pl.BlockSpec((tm, main_w), lambda i: (i, 0)),
                  pl.BlockSpec((tm, MEM_W), lambda i: (i, 0)),
                  pl.BlockSpec((None, main_w, d), lambda i: (layer, 0, 0),
                               pipeline_mode=pl.Buffered(1)),
                  pl.BlockSpec((None, MEM_W, d), lambda i: (layer, main_w // MEM_W, 0),
                               pipeline_mode=pl.Buffered(1))],
        out_specs=pl.BlockSpec((tm, d), lambda i: (i, 0)),
        out_shape=jax.ShapeDtypeStruct((n, d), F32),
        compiler_params=_params("parallel"),
        name="out_proj",
    )(x, o, mo, w_out, w_out)


def _pad_cols(w, width):
    return jnp.pad(w, [(0, 0)] * (w.ndim - 1) + [(0, width - w.shape[-1])])


def _pad_gla_heads(w, hk):
    lead = w.shape[:-1]
    w = w.reshape(lead + (GLA_HEADS, hk))
    w = jnp.pad(w, [(0, 0)] * len(lead) + [(0, 0), (0, GLA_HK_PAD - hk)])
    return w.reshape(lead + (GLA_HEADS * GLA_HK_PAD,))


def _forward(x, mem, ffn_norm, ffn_w1, ffn_w3, ffn_w2, mix_norm, mem_norm, w_mem_kv,
             mem_q_norm, mem_k_norm, w_out, a_w_in, a_w_gate_up, a_b_gate, a_out_norm,
             b_w_in, b_q_norm, kv_norm, w_kv, b_f, k_norm, *, tiles):
    bsz, seq, d = x.shape
    n = bsz * seq
    depth = ffn_norm.shape[0]
    n_a = a_w_in.shape[0]
    main_w = d - MEM_W
    gla_key = main_w // 2
    hk = gla_key // GLA_HEADS
    fox_heads = main_w // HEAD_DIM
    tn = tiles["proj_tn"]
    attn_scale = HEAD_DIM ** -0.5

    w1 = ffn_w1.astype(BF16)
    w3 = ffn_w3.astype(BF16)
    w2 = ffn_w2
    w_out_b = w_out.astype(BF16)
    w_mem_b = w_mem_kv.astype(BF16)
    w_b = b_w_in.astype(BF16)
    w_kv_b = jnp.concatenate([w_kv[:, :2 * main_w], _pad_cols(w_kv[:, 2 * main_w:], tn)],
                             axis=1).astype(BF16)[None]
    c = np.cumsum([0, gla_key, gla_key, main_w, GLA_GATE_RANK, main_w, MEM_W])
    kw = GLA_HEADS * GLA_HK_PAD
    w_a = jnp.concatenate([
        _pad_gla_heads(a_w_in[..., c[0]:c[1]], hk),
        _pad_gla_heads(a_w_in[..., c[1]:c[2]], hk),
        a_w_in[..., c[2]:c[3]],
        a_w_in[..., c[4]:c[5]],
        a_w_in[..., c[5]:c[6]],
    ], axis=-1).astype(BF16)
    w_lr = _pad_cols(a_w_in[..., c[3]:c[4]], LANES).astype(BF16)
    w_up = jnp.pad(_pad_gla_heads(a_w_gate_up, hk),
                   ((0, 0), (0, LANES - GLA_GATE_RANK), (0, 0))).astype(BF16)
    b_gate = _pad_gla_heads(a_b_gate, hk)[:, None, :]
    x2 = x.reshape(n, d)
    mem2 = mem.reshape(bsz * mem.shape[1], d)

    def ffn(x2, l, s):
        return _ffn(x2, ffn_norm[l, s], w1, w3, w2, l, s,
                    tm=tiles["ffn_tm"], tf=tiles["ffn_tf"])

    k_sh = v_sh = c2 = k_decay = None
    for l in range(depth):
        if l == n_a:
            bias = _pad_cols(b_f.reshape(1, fox_heads), LANES)
            k_sh, v_sh, logf = _norm_proj(
                x2, kv_norm, w_kv_b, 0,
                [(main_w // tn, tn, BF16, _head_norm_ep(0, 1.0)),
                 (main_w // tn, tn, BF16, _plain_ep),
                 (1, LANES, F32, _forget_gate_ep(1))],
                [k_norm.reshape(1, HEAD_DIM), bias],
                tm=tiles["proj_tm"], tn=tn, name="kv_proj")
            c2, k_decay = _seq_cumsum(logf, bsz, seq, fox_heads, ts=tiles["cum_ts"])

        x2 = ffn(x2, l, 0)

        mk, mv = _norm_proj(
            mem2, mem_norm[l], w_mem_b, l,
            [(MEM_W // tn, tn, BF16, _head_norm_ep(0, 1.0)),
             (MEM_W // tn, tn, BF16, _plain_ep)],
            [mem_k_norm[l].reshape(1, HEAD_DIM)],
            tm=tiles["mem_tm"], tn=tn, name="mem_kv_proj")
        assert MEM_W == tn, "the memory-attention epilogue needs all memory heads in one tile"
        mem_rows = [(mk, mem.shape[1], seq // tiles["proj_tm"]),
                    (mv, mem.shape[1], seq // tiles["proj_tm"])]

        if l < n_a:
            q, k, v, g, mo, la = _norm_proj(
                x2, mix_norm[l], w_a, l,
                [(kw // tn, tn, BF16, _scale_ep(hk ** -0.5)),
                 (kw // tn, tn, F32, _plain_ep),
                 (main_w // tn, tn, BF16, _plain_ep),
                 (main_w // tn, tn, BF16, _plain_ep),
                 (1, tn, BF16, _mem_attn_ep(0, 3, 4))],
                [mem_q_norm[l].reshape(1, HEAD_DIM), w_up[l], b_gate[l]],
                tm=tiles["proj_tm"], tn=tn, name="gla_in_proj", row_extras=mem_rows,
                lowrank=(w_lr[l], kw, F32, _gla_gate_ep(1, 2)))
            o = _gla(q, k, la, v, g, a_out_norm[l], bsz, seq, n_chunk=tiles["gla_chunks"])
        else:
            jl = l - n_a
            q, g, mo = _norm_proj(
                x2, mix_norm[l], w_b, jl,
                [(main_w // tn, tn, BF16, _head_norm_ep(0, attn_scale * LOG2E)),
                 (main_w // tn, tn, BF16, _plain_ep),
                 (1, tn, BF16, _mem_attn_ep(1, 2, 3))],
                [b_q_norm[jl].reshape(1, HEAD_DIM), mem_q_norm[l].reshape(1, HEAD_DIM)],
                tm=tiles["proj_tm"], tn=tn, name="fox_in_proj", row_extras=mem_rows)
            o = _fox(q.reshape(bsz, seq, main_w), k_sh.reshape(bsz, seq, main_w),
                     v_sh.reshape(bsz, seq, main_w), k_decay, c2,
                     g.reshape(bsz, seq, main_w), t=tiles["fox_t"], rq=tiles["fox_rq"], hp=tiles["fox_hp"])
            o = o.reshape(n, main_w)

        x2 = _out_proj(x2, o, mo, w_out_b, l, tm=tiles["out_tm"])
        x2 = ffn(x2, l, 1)
    return x2.reshape(bsz, seq, d)


_TILES = dict(ffn_tm=1024, ffn_tf=512, proj_tm=1024, proj_tn=512, mem_tm=512, cum_ts=512,
              gla_chunks=16, fox_t=1024, fox_rq=256, fox_hp=4, out_tm=1024)


def kernel(x, mem, ffn_norm, ffn_w1, ffn_w3, ffn_w2, mix_norm, mem_norm, w_mem_kv, mem_q_norm, mem_k_norm, w_out, a_w_in, a_w_gate_up, a_b_gate, a_out_norm, b_w_in, b_q_norm, kv_norm, w_kv, b_f, k_norm):
    return _forward(x, mem, ffn_norm, ffn_w1, ffn_w3, ffn_w2, mix_norm, mem_norm, w_mem_kv,
                    mem_q_norm, mem_k_norm, w_out, a_w_in, a_w_gate_up, a_b_gate, a_out_norm,
                    b_w_in, b_q_norm, kv_norm, w_kv, b_f, k_norm, tiles=_TILES)
```
